```python
import jax, jax.numpy as jnp
from jax import lax
import numpy as np

D_MODEL = 1024
BATCH = 8
SEQ = 4096
DEPTH = 2

CHUNK = 64
BRANCH_W = 512
N_BRANCH = 4
NORM_EPS = 1e-6
SQRT_EPS = 1e-12
HG_HEADS = 4
HG_DK = 128
HG_DV = BRANCH_W // HG_HEADS
GLA_HEADS = 4
GLA_DK = 64
GLA_DV = BRANCH_W // GLA_HEADS
GLA_RANK = 16
GLA_TAU = 16.0
SG_GROUPS = 4
SG_LEN = 128
SG_GW = BRANCH_W // SG_GROUPS
LRU_HEADS = 4
LRU_HD = BRANCH_W // LRU_HEADS
CONV_WIDTH = 4
RG_C = 8.0

IN_SPLITS = (HG_HEADS * HG_DK, HG_HEADS * HG_DK, BRANCH_W, BRANCH_W,
             GLA_HEADS * GLA_DK, GLA_HEADS * GLA_DK, BRANCH_W, BRANCH_W,
             GLA_RANK,
             BRANCH_W, BRANCH_W, BRANCH_W,
             BRANCH_W, BRANCH_W,
             N_BRANCH * D_MODEL)
N_IN = sum(IN_SPLITS)

kernel_name = 'hybrid_hgrn2_gla_sgmlp_rglru_encoder'


def rms_norm(x, w):
    xf = x.astype(jnp.float32)
    y = xf * lax.rsqrt(jnp.mean(xf * xf, axis=-1, keepdims=True) + NORM_EPS)
    return (y * w.astype(jnp.float32)).astype(x.dtype)


def head_rms_norm(o, w):
    y = o * lax.rsqrt(jnp.mean(o * o, axis=-1, keepdims=True) + NORM_EPS) * w.astype(jnp.float32)
    return y.reshape(o.shape[0], o.shape[1], -1)


def chunk_gated_linear_attention(q, k, v, log_decay):
    bsz, seqlen, nh, dk = q.shape
    dv = v.shape[-1]
    n = seqlen // CHUNK

    def to_chunks(t):
        return t.astype(jnp.float32).reshape(bsz, n, CHUNK, nh, t.shape[-1]).transpose(1, 0, 3, 2, 4)

    qc, kc, vc, gc = to_chunks(q), to_chunks(k), to_chunks(v), to_chunks(log_decay)
    bc = jnp.cumsum(gc, axis=3)
    causal = jnp.tril(jnp.ones((CHUNK, CHUNK), dtype=bool))[:, :, None]
    causal_f = causal.astype(jnp.float32)

    def step(state, inp):
        qi, ki, vi, bi = inp
        diff = bi[:, :, :, None, :] - bi[:, :, None, :, :]
        decay = jnp.exp(jnp.where(causal, diff, 0.0)) * causal_f
        scores = jnp.sum(qi[:, :, :, None, :] * decay * ki[:, :, None, :, :], axis=-1)
        o = (jnp.einsum('bhij,bhjv->bhiv', scores, vi)
             + jnp.einsum('bhik,bhkv->bhiv', qi * jnp.exp(bi), state))
        b_last = bi[:, :, -1:, :]
        state = (jnp.exp(b_last[:, :, 0, :, None]) * state
                 + jnp.einsum('bhjk,bhjv->bhkv', ki * jnp.exp(b_last - bi), vi))
        return state, o

    s0 = jnp.zeros((bsz, nh, dk, dv), jnp.float32)
    _, o = lax.scan(step, s0, (qc, kc, vc, bc))
    return o.transpose(1, 0, 3, 2, 4).reshape(bsz, seqlen, nh, dv)


def hgrn2_branch(q, f_logit, inp, gate, lb, norm_w):
    bsz, seqlen, _ = q.shape
    z = f_logit.astype(jnp.float32).reshape(bsz, seqlen, HG_HEADS, HG_DK)
    lbh = lb.reshape(HG_HEADS, HG_DK)
    forget = lbh + (1.0 - lbh) * jax.nn.sigmoid(z)
    log_f = jnp.log(forget)
    key = (1.0 - lbh) * jax.nn.sigmoid(-z)
    qh = q.reshape(bsz, seqlen, HG_HEADS, HG_DK) * (HG_DK ** -0.5)
    vh = inp.reshape(bsz, seqlen, HG_HEADS, HG_DV)
    o = chunk_gated_linear_attention(qh, key, vh, log_f)
    return head_rms_norm(o, norm_w).astype(q.dtype) * jax.nn.silu(gate)


def gla_branch(q, k, v, gate, lowrank, gk_w, gk_b, norm_w):
    bsz, seqlen, _ = q.shape
    gk = (lowrank @ gk_w + gk_b).astype(jnp.float32)
    log_alpha = (jax.nn.log_sigmoid(gk) / GLA_TAU).reshape(bsz, seqlen, GLA_HEADS, GLA_DK)
    qh = q.reshape(bsz, seqlen, GLA_HEADS, GLA_DK) * (GLA_DK ** -0.5)
    kh = k.reshape(bsz, seqlen, GLA_HEADS, GLA_DK)
    vh = v.reshape(bsz, seqlen, GLA_HEADS, GLA_DV)
    o = chunk_gated_linear_attention(qh, kh, vh, log_alpha)
    return head_rms_norm(o, norm_w).astype(q.dtype) * jax.nn.silu(gate)


def spatial_gating_branch(u, v, gate, ln_w, ln_b, w_s, b_s):
    bsz, seqlen, _ = u.shape
    u = jax.nn.gelu(u, approximate=False)
    v = jax.nn.gelu(v, approximate=False)
    vf = v.astype(jnp.float32)
    mu = jnp.mean(vf, axis=-1, keepdims=True)
    var = jnp.mean(jnp.square(vf - mu), axis=-1, keepdims=True)
    vn = ((vf - mu) * lax.rsqrt(var + NORM_EPS) * ln_w.astype(jnp.float32) + ln_b.astype(jnp.float32)).astype(u.dtype)
    vr = vn.reshape(bsz, seqlen // SG_LEN, SG_LEN, SG_GROUPS, SG_GW)
    blk = jnp.arange(SG_LEN) // CHUNK
    mask = (blk[None, :] <= blk[:, None]).astype(w_s.dtype)
    mixed = jnp.einsum('gij,bnjgc->bnigc', w_s * mask, vr) + b_s.T[None, None, :, :, None]
    return u * mixed.reshape(bsz, seqlen, BRANCH_W) * jax.nn.silu(gate)


def rglru_branch(xr, gate, conv_w, conv_b, w_a, b_a, w_x, b_x, lam):
    bsz, seqlen, _ = xr.shape
    xp = jnp.pad(xr, ((0, 0), (CONV_WIDTH - 1, 0), (0, 0)))
    xc = conv_b
    for tap in range(CONV_WIDTH):
        xc = xc + xp[:, tap:tap + seqlen] * conv_w[tap]
    xh = xc.reshape(bsz, seqlen, LRU_HEADS, LRU_HD)
    r = jax.nn.sigmoid(jnp.einsum('bshi,hij->bshj', xh, w_a).reshape(bsz, seqlen, BRANCH_W) + b_a)
    ig = jax.nn.sigmoid(jnp.einsum('bshi,hij->bshj', xh, w_x).reshape(bsz, seqlen, BRANCH_W) + b_x)
    log_a = -RG_C * r.astype(jnp.float32) * jax.nn.softplus(-lam.astype(jnp.float32))
    a = jnp.exp(log_a)
    bterm = jnp.sqrt(jnp.maximum(-jnp.expm1(2.0 * log_a), SQRT_EPS)) * (ig * xc).astype(jnp.float32)

    def combine(left, right):
        a1, b1 = left
        a2, b2 = right
        return a1 * a2, a2 * b1 + b2

    _, h = lax.associative_scan(combine, (a, bterm), axis=1)
    return h.astype(xr.dtype) * jax.nn.silu(gate)


def setup_inputs(seed: int = 0) -> dict:
    key = jax.random.key(seed)
    ks = jax.random.split(key, 24)
    f32 = jnp.float32
    nrm = lambda k, shape, s: jax.random.normal(k, shape, f32) * s
    u_a = jax.random.uniform(ks[20], (DEPTH, BRANCH_W), f32, 0.9, 0.999)
    sig_l = u_a ** (1.0 / RG_C)
    return {
        'x': jax.random.normal(ks[0], (BATCH, SEQ, D_MODEL), f32),
        'norm_w': 1.0 + nrm(ks[1], (DEPTH, D_MODEL), 0.01),
        'w_in': nrm(ks[2], (DEPTH, D_MODEL, N_IN), D_MODEL ** -0.5),
        'hg_lb_logits': nrm(ks[3], (DEPTH, HG_HEADS * HG_DK), 1.0),
        'hg_norm_w': 1.0 + nrm(ks[4], (DEPTH, HG_DV), 0.01),
        'gla_gk_w': nrm(ks[5], (DEPTH, GLA_RANK, GLA_HEADS * GLA_DK), GLA_RANK ** -0.5),
        'gla_gk_b': nrm(ks[6], (DEPTH, GLA_HEADS * GLA_DK), 0.1),
        'gla_norm_w': 1.0 + nrm(ks[7], (DEPTH, GLA_DV), 0.01),
        'sg_ln_w': 1.0 + nrm(ks[8], (DEPTH, BRANCH_W), 0.01),
        'sg_ln_b': nrm(ks[9], (DEPTH, BRANCH_W), 0.01),
        'sg_w': nrm(ks[10], (DEPTH, SG_GROUPS, SG_LEN, SG_LEN), 0.5 * SG_LEN ** -0.5),
        'sg_b': 1.0 + nrm(ks[11], (DEPTH, SG_GROUPS, SG_LEN), 0.1),
        'lru_conv_w': nrm(ks[12], (DEPTH, CONV_WIDTH, BRANCH_W), CONV_WIDTH ** -0.5),
        'lru_conv_b': nrm(ks[13], (DEPTH, BRANCH_W), 0.01),
        'lru_w_a': nrm(ks[14], (DEPTH, LRU_HEADS, LRU_HD, LRU_HD), LRU_HD ** -0.5),
        'lru_b_a': nrm(ks[15], (DEPTH, BRANCH_W), 0.1),
        'lru_w_x': nrm(ks[16], (DEPTH, LRU_HEADS, LRU_HD, LRU_HD), LRU_HD ** -0.5),
        'lru_b_x': nrm(ks[17], (DEPTH, BRANCH_W), 0.1),
        'lru_lambda': jnp.log(sig_l) - jnp.log1p(-sig_l),
        'w_branch': nrm(ks[18], (DEPTH, N_BRANCH, BRANCH_W, D_MODEL), BRANCH_W ** -0.5),
        'w_out': nrm(ks[19], (DEPTH, D_MODEL, D_MODEL), D_MODEL ** -0.5),
        'final_norm_w': 1.0 + nrm(ks[21], (D_MODEL,), 0.01),
    }


def reference(x, norm_w, w_in, hg_lb_logits, hg_norm_w, gla_gk_w, gla_gk_b, gla_norm_w,
              sg_ln_w, sg_ln_b, sg_w, sg_b, lru_conv_w, lru_conv_b, lru_w_a, lru_b_a,
              lru_w_x, lru_b_x, lru_lambda, w_branch, w_out, final_norm_w):
    bsz, seqlen, _ = x.shape
    p = jax.nn.softmax(hg_lb_logits.astype(jnp.float32), axis=0)
    lower_bounds = jnp.cumsum(p, axis=0) - p[0:1]
    offsets = [int(o) for o in np.cumsum(IN_SPLITS)[:-1]]
    h = x
    for l in range(DEPTH):
        z = rms_norm(h, norm_w[l])
        proj = z @ w_in[l]
        (hq, hf, hi, hg, gq, gk, gv, gg, glr, su, sv, sgt, rx, rg, mg) = jnp.split(proj, offsets, axis=-1)
        y_a = hgrn2_branch(hq, hf, hi, hg, lower_bounds[l], hg_norm_w[l])
        y_b = gla_branch(gq, gk, gv, gg, glr, gla_gk_w[l], gla_gk_b[l], gla_norm_w[l])
        y_c = spatial_gating_branch(su, sv, sgt, sg_ln_w[l], sg_ln_b[l], sg_w[l], sg_b[l])
        y_d = rglru_branch(rx, rg, lru_conv_w[l], lru_conv_b[l], lru_w_a[l], lru_b_a[l],
                           lru_w_x[l], lru_b_x[l], lru_lambda[l])
        gates = jax.nn.sigmoid(mg.reshape(bsz, seqlen, N_BRANCH, D_MODEL))
        branches = (y_a, y_b, y_c, y_d)
        merged = gates[:, :, 0] * (branches[0] @ w_branch[l, 0])
        for bi in range(1, N_BRANCH):
            merged = merged + gates[:, :, bi] * (branches[bi] @ w_branch[l, bi])
        h = h + merged @ w_out[l]
    return rms_norm(h, final_norm_w)
```

```python
import functools

import numpy as np
import jax
import jax.numpy as jnp
from jax import lax
from jax.experimental import pallas as pl
from jax.experimental.pallas import tpu as pltpu

F32 = jnp.float32
BF16 = jnp.bfloat16

D_MODEL = 1024
DEPTH = 2
CHUNK = 64
BRANCH_W = 512
N_BRANCH = 4
NORM_EPS = 1e-6
SQRT_EPS = 1e-12
HG_HEADS = 4
HG_DK = 128
GLA_HEADS = 4
GLA_DK = 64
GLA_RANK = 16
GLA_TAU = 16.0
SG_GROUPS = 4
SG_LEN = 128
LRU_HEADS = 4
LRU_HD = 128
CONV_WIDTH = 4
RG_C = 8.0

LANES = 128
SUBLANES = 8
VMEM_LIMIT_BYTES = 56 * 1024 * 1024

OFF_A = 0
OFF_B = 2048
OFF_LR = 3584
OFF_C = 3600
OFF_D = 5136
OFF_MG = 6160
N_IN = 10256

LEVELS = (32, 16, 8, 4, 2, 1)
N_DECAY_BLOCKS = 2 + len(LEVELS)


def _decay_matrix():
    c = CHUNK
    i = np.arange(c)[:, None]
    j = np.arange(c)[None, :]
    blocks = [(j <= i), (j > i)]
    for s in LEVELS:
        ref = (i // (2 * s)) * (2 * s) + s - 1
        in_b = i > ref
        blocks.append(np.where(in_b, (j > ref) & (j <= i), (j > i) & (j <= ref)))
    m = np.concatenate(blocks, axis=0).astype(np.float32)
    return np.concatenate([m, m, m], axis=1)


def _level_masks():
    c = CHUNK
    i = np.arange(c)[:, None]
    j = np.arange(c)[None, :]
    masks = [(i == j)]
    for s in LEVELS:
        same = (i // (2 * s)) == (j // (2 * s))
        masks.append(same & ((i // s) % 2 == 1) & ((j // s) % 2 == 0))
    return np.stack(masks).astype(np.float32)


def _sigmoid(x):
    return 0.5 + 0.5 * jnp.tanh(0.5 * x)


def _silu(x):
    return x * _sigmoid(x)


def _dot(a, b):
    return jnp.dot(a, b, preferred_element_type=F32)


def _dot_nt(a, b):
    return lax.dot_general(a, b, (((1,), (1,)), ((), ())), preferred_element_type=F32)


def _dot_tn(a, b):
    return lax.dot_general(a, b, (((0,), (0,)), ((), ())), preferred_element_type=F32)


def _split3(x):
    hi = x.astype(BF16)
    r1 = x - hi.astype(F32)
    mid = r1.astype(BF16)
    lo = (r1 - mid.astype(F32)).astype(BF16)
    return jnp.concatenate([hi, mid, lo], axis=0)


def _rms_norm_rows(x, w):
    return x * lax.rsqrt(jnp.mean(x * x, axis=-1, keepdims=True) + NORM_EPS) * w


def _params(n_grid_dims):
    return pltpu.CompilerParams(
        dimension_semantics=("arbitrary",) * n_grid_dims,
        vmem_limit_bytes=VMEM_LIMIT_BYTES,
    )


def _full_spec(shape):
    return pl.BlockSpec(shape, lambda *_: (0,) * len(shape))


def _row_spec(ts, width):
    return pl.BlockSpec((1, ts, width), lambda b, s: (b, s, 0))


def _norm_kernel(x_ref, w_ref, z_ref):
    z_ref[0] = _rms_norm_rows(x_ref[0], w_ref[...]).astype(BF16)


def _first_norm(x, w, ts):
    bsz, seqlen, d = x.shape
    return pl.pallas_call(
        _norm_kernel,
        grid=(bsz, seqlen // ts),
        in_specs=[_row_spec(ts, d), _full_spec((1, d))],
        out_specs=_row_spec(ts, d),
        out_shape=jax.ShapeDtypeStruct((bsz, seqlen, d), BF16),
        compiler_params=_params(2),
        name="first_norm",
    )(x, w.reshape(1, d))


def _chunk_attention(log_decay, q_units, k_units, v_heads, heads_per_unit,
                     dmat_ref, masks_ref, e_ref, state_ref):
    c = CHUNK
    e_ref[...] = jnp.exp(_dot(dmat_ref[...], _split3(log_decay)))
    n_heads = len(v_heads)
    lane = lax.broadcasted_iota(jnp.int32, (c, LANES), 1)
    head_w = LANES // heads_per_unit
    outs = []
    for h in range(n_heads):
        u = h // heads_per_unit
        lanes = slice(u * LANES, (u + 1) * LANES)
        q = q_units[u]
        k = k_units[u]
        if heads_per_unit > 1:
            sub = h % heads_per_unit
            in_head = (lane >= sub * head_w) & (lane < (sub + 1) * head_w)
            q = jnp.where(in_head, q, 0.0)
            k_own = jnp.where(in_head, k, 0.0)
        else:
            k_own = k
        v16 = v_heads[h].astype(BF16)
        scores = masks_ref[0] * _dot_nt(q.astype(BF16), k.astype(BF16))
        for lvl in range(len(LEVELS)):
            e = e_ref[(2 + lvl) * c:(3 + lvl) * c, lanes]
            scores = scores + masks_ref[1 + lvl] * _dot_nt((q * e).astype(BF16), (k * e).astype(BF16))
        e_in = e_ref[0:c, lanes]
        e_out = e_ref[c:2 * c, lanes]
        e_last = e_ref[c - 1:c, lanes]
        st = state_ref[h]
        o = _dot(scores.astype(BF16), v16) + _dot_nt((q * e_in).astype(BF16), st.astype(BF16))
        state_ref[h] = st * e_last + _dot_tn(v16, (k_own * e_out).astype(BF16))
        outs.append(o)
    return outs


def _head_norm_gate(o, norm_w, gate):
    y = o * lax.rsqrt(jnp.mean(o * o, axis=-1, keepdims=True) + NORM_EPS) * norm_w
    return y * _silu(gate)


def _hgrn2_kernel(layer, z_ref, w_ref, lbl_ref, nw_ref, dmat_ref, masks_ref, y_ref,
                  proj_ref, e_ref, state_ref):
    @pl.when(pl.program_id(1) == 0)
    def _():
        state_ref[...] = jnp.zeros_like(state_ref)

    proj_ref[...] = _dot(z_ref[0], w_ref[...])

    logits = [lbl_ref[i:i + 1, :] for i in range(DEPTH)]
    mx = functools.reduce(jnp.maximum, logits)
    ex = [jnp.exp(t - mx) for t in logits]
    tot = functools.reduce(lambda a, b: a + b, ex)
    lb = functools.reduce(lambda a, b: a + b, ex[:layer + 1]) / tot - ex[0] / tot
    nw = nw_ref[...]
    w = BRANCH_W
    ts = proj_ref.shape[0]

    def body(ci, carry):
        r0 = pl.multiple_of(ci * CHUNK, CHUNK)
        rows = pl.ds(r0, CHUNK)
        q = proj_ref[rows, 0:w] * (HG_DK ** -0.5)
        zf = proj_ref[rows, w:2 * w]
        vin = proj_ref[rows, 2 * w:3 * w]
        gate = proj_ref[rows, 3 * w:4 * w]
        en = jnp.exp(-jnp.abs(zf))
        r = 1.0 / (1.0 + en)
        er = en * r
        pos = zf >= 0.0
        sig = jnp.where(pos, r, er)
        nsig = jnp.where(pos, er, r)
        log_f = jnp.log(lb + (1.0 - lb) * sig)
        key = (1.0 - lb) * nsig
        units = [slice(h * LANES, (h + 1) * LANES) for h in range(HG_HEADS)]
        outs = _chunk_attention(
            log_f, [q[:, u] for u in units], [key[:, u] for u in units],
            [vin[:, u] for u in units], 1, dmat_ref, masks_ref, e_ref, state_ref)
        for h, u in enumerate(units):
            y_ref[0, rows, u] = _head_norm_gate(outs[h], nw, gate[:, u]).astype(BF16)
        return carry

    lax.fori_loop(0, ts // CHUNK, body, 0)


def _hgrn2(z, w_a, lb_logits, norm_w, dmat, masks, layer, ts):
    bsz, seqlen, d = z.shape
    n_cols = w_a.shape[1]
    return pl.pallas_call(
        functools.partial(_hgrn2_kernel, layer),
        grid=(bsz, seqlen // ts),
        in_specs=[_row_spec(ts, d), _full_spec((d, n_cols)), _full_spec(lb_logits.shape),
                  _full_spec((1, LANES)), _full_spec(dmat.shape), _full_spec(masks.shape)],
        out_specs=_row_spec(ts, BRANCH_W),
        out_shape=jax.ShapeDtypeStruct((bsz, seqlen, BRANCH_W), BF16),
        scratch_shapes=[pltpu.VMEM((ts, n_cols), F32),
                        pltpu.VMEM((N_DECAY_BLOCKS * CHUNK, BRANCH_W), F32),
                        pltpu.VMEM((HG_HEADS, LANES, LANES), F32)],
        compiler_params=_params(2),
        name="hgrn2_mixer",
    )(z, w_a, lb_logits, norm_w.reshape(1, LANES), dmat, masks)


def _gla_kernel(z_ref, w_ref, gkw_ref, gkb_ref, nw_ref, dmat_ref, masks_ref, y_ref,
                proj_ref, e_ref, state_ref):
    @pl.when(pl.program_id(1) == 0)
    def _():
        state_ref[...] = jnp.zeros_like(state_ref)

    proj_ref[...] = _dot(z_ref[0], w_ref[...])
    nw = nw_ref[...]
    gkb = gkb_ref[...]
    qk_w = GLA_HEADS * GLA_DK
    w = BRANCH_W
    ts = proj_ref.shape[0]

    def body(ci, carry):
        r0 = pl.multiple_of(ci * CHUNK, CHUNK)
        rows = pl.ds(r0, CHUNK)
        q = proj_ref[rows, 0:qk_w] * (GLA_DK ** -0.5)
        k = proj_ref[rows, qk_w:2 * qk_w]
        v = proj_ref[rows, 2 * qk_w:2 * qk_w + w]
        gate = proj_ref[rows, 2 * qk_w + w:2 * qk_w + 2 * w]
        lowrank = proj_ref[rows, 2 * qk_w + 2 * w:2 * qk_w + 2 * w + LANES]
        gk = _dot(lowrank.astype(BF16), gkw_ref[...]) + gkb
        log_alpha = (jnp.minimum(gk, 0.0) - jnp.log(1.0 + jnp.exp(-jnp.abs(gk)))) * (1.0 / GLA_TAU)
        units = [slice(u * LANES, (u + 1) * LANES) for u in range(qk_w // LANES)]
        heads = [slice(h * LANES, (h + 1) * LANES) for h in range(GLA_HEADS)]
        outs = _chunk_attention(
            log_alpha, [q[:, u] for u in units], [k[:, u] for u in units],
            [v[:, hs] for hs in heads], LANES // GLA_DK, dmat_ref, masks_ref, e_ref, state_ref)
        for h, hs in enumerate(heads):
            y_ref[0, rows, hs] = _head_norm_gate(outs[h], nw, gate[:, hs]).astype(BF16)
        return carry

    lax.fori_loop(0, ts // CHUNK, body, 0)


def _gla(z, w_b, gk_w, gk_b, norm_w, dmat, masks, ts):
    bsz, seqlen, d = z.shape
    n_cols = w_b.shape[1]
    qk_w = GLA_HEADS * GLA_DK
    return pl.pallas_call(
        _gla_kernel,
        grid=(bsz, seqlen // ts),
        in_specs=[_row_spec(ts, d), _full_spec((d, n_cols)), _full_spec((LANES, qk_w)),
                  _full_spec((1, qk_w)), _full_spec((1, LANES)), _full_spec(dmat.shape),
                  _full_spec(masks.shape)],
        out_specs=_row_spec(ts, BRANCH_W),
        out_shape=jax.ShapeDtypeStruct((bsz, seqlen, BRANCH_W), BF16),
        scratch_shapes=[pltpu.VMEM((ts, n_cols), F32),
                        pltpu.VMEM((N_DECAY_BLOCKS * CHUNK, qk_w), F32),
                        pltpu.VMEM((GLA_HEADS, LANES, LANES), F32)],
        compiler_params=_params(2),
        name="gla_mixer",
    )(z, w_b, gk_w, gk_b.reshape(1, qk_w), norm_w.reshape(1, LANES), dmat, masks)


def _gelu(x):
    return 0.5 * x * (1.0 + lax.erf(x * (2.0 ** -0.5)))


def _sg_kernel(z_ref, w_ref, lnw_ref, lnb_ref, ws_ref, bias_ref, y_ref):
    w = BRANCH_W
    proj = _dot(z_ref[0], w_ref[...])
    ts = proj.shape[0]
    u = _gelu(proj[:, 0:w])
    vf = _gelu(proj[:, w:2 * w])
    gate = proj[:, 2 * w:3 * w]
    mu = jnp.mean(vf, axis=-1, keepdims=True)
    dv = vf - mu
    var = jnp.mean(dv * dv, axis=-1, keepdims=True)
    vn = (dv * lax.rsqrt(var + NORM_EPS) * lnw_ref[...] + lnb_ref[...]).astype(BF16)
    ri = lax.broadcasted_iota(jnp.int32, (SG_LEN, SG_LEN), 0)
    cj = lax.broadcasted_iota(jnp.int32, (SG_LEN, SG_LEN), 1)
    allowed = (cj < CHUNK) | (ri >= CHUNK)
    wm = [jnp.where(allowed, ws_ref[g], 0.0).astype(BF16) for g in range(SG_GROUPS)]
    bias = bias_ref[...]
    ug = u * _silu(gate)
    for blk in range(ts // SG_LEN):
        rows = slice(blk * SG_LEN, (blk + 1) * SG_LEN)
        for g in range(SG_GROUPS):
            cols = slice(g * LANES, (g + 1) * LANES)
            mixed = _dot(wm[g], vn[rows, cols]) + bias[:, cols]
            y_ref[0, rows, cols] = (ug[rows, cols] * mixed).astype(BF16)


def _spatial_gating(z, w_c, ln_w, ln_b, w_s, bias, ts):
    bsz, seqlen, d = z.shape
    n_cols = w_c.shape[1]
    return pl.pallas_call(
        _sg_kernel,
        grid=(bsz, seqlen // ts),
        in_specs=[_row_spec(ts, d), _full_spec((d, n_cols)), _full_spec((1, BRANCH_W)),
                  _full_spec((1, BRANCH_W)), _full_spec(w_s.shape), _full_spec(bias.shape)],
        out_specs=_row_spec(ts, BRANCH_W),
        out_shape=jax.ShapeDtypeStruct((bsz, seqlen, BRANCH_W), BF16),
        compiler_params=_params(2),
        name="spatial_gating_mixer",
    )(z, w_c, ln_w.reshape(1, BRANCH_W), ln_b.reshape(1, BRANCH_W), w_s, bias)


def _expm1_nonpos(x):
    series = x * (1.0 + x * (1.0 / 2) * (1.0 + x * (1.0 / 3) * (1.0 + x * (1.0 / 4) * (
        1.0 + x * (1.0 / 5) * (1.0 + x * (1.0 / 6) * (1.0 + x * (1.0 / 7)))))))
    return jnp.where(x > -0.25, series, jnp.exp(x) - 1.0)


def _lru_kernel(z_ref, w_ref, cw_ref, cb_ref, wax_ref, ba_ref, bx_ref, lam_ref, y_ref,
                xext_ref, hprev_ref):
    w = BRANCH_W
    pad = SUBLANES

    @pl.when(pl.program_id(1) == 0)
    def _():
        xext_ref[0:pad, :] = jnp.zeros((pad, w), F32)
        hprev_ref[...] = jnp.zeros_like(hprev_ref)

    proj = _dot(z_ref[0], w_ref[...])
    ts = proj.shape[0]
    xr = proj[:, 0:w]
    gate = proj[:, w:2 * w]
    xext_ref[pad:pad + ts, :] = xr
    xc = cb_ref[...]
    for tap in range(CONV_WIDTH):
        back = CONV_WIDTH - 1 - tap
        xs = xr if back == 0 else xext_ref[pad - back:pad - back + ts, :]
        xc = xc + xs * cw_ref[tap:tap + 1, :]
    xext_ref[0:pad, :] = xr[ts - pad:ts, :]

    xc16 = xc.astype(BF16)
    ra, ix = [], []
    for h in range(LRU_HEADS):
        cols = slice(h * LRU_HD, (h + 1) * LRU_HD)
        both = _dot(xc16[:, cols], wax_ref[h])
        ra.append(both[:, 0:LRU_HD])
        ix.append(both[:, LRU_HD:2 * LRU_HD])
    r = _sigmoid(jnp.concatenate(ra, axis=1) + ba_ref[...])
    ig = _sigmoid(jnp.concatenate(ix, axis=1) + bx_ref[...])
    nlam = -lam_ref[...]
    softplus = jnp.maximum(nlam, 0.0) + jnp.log(1.0 + jnp.exp(-jnp.abs(nlam)))
    log_a = (-RG_C) * r * softplus
    a = jnp.exp(log_a)
    bt = jnp.sqrt(jnp.maximum(-_expm1_nonpos(2.0 * log_a), SQRT_EPS)) * (ig * xc)

    row = lax.broadcasted_iota(jnp.int32, (ts, w), 0)
    d = 1
    while d < ts:
        a_sh = jnp.where(row < d, 1.0, pltpu.roll(a, d, axis=0))
        b_sh = jnp.where(row < d, 0.0, pltpu.roll(bt, d, axis=0))
        bt = bt + a * b_sh
        a = a * a_sh
        d *= 2
    hseq = bt + a * hprev_ref[...]
    hprev_ref[...] = hseq[ts - 1:ts, :]
    y_ref[0] = (hseq * _silu(gate)).astype(BF16)


def _rglru(z, w_d, conv_w, conv_b, wax, b_a, b_x, lam, ts):
    bsz, seqlen, d = z.shape
    n_cols = w_d.shape[1]
    w = BRANCH_W
    return pl.pallas_call(
        _lru_kernel,
        grid=(bsz, seqlen // ts),
        in_specs=[_row_spec(ts, d), _full_spec((d, n_cols)), _full_spec((CONV_WIDTH, w)),
                  _full_spec((1, w)), _full_spec(wax.shape), _full_spec((1, w)),
                  _full_spec((1, w)), _full_spec((1, w))],
        out_specs=_row_spec(ts, w),
        out_shape=jax.ShapeDtypeStruct((bsz, seqlen, w), BF16),
        scratch_shapes=[pltpu.VMEM((ts + SUBLANES, w), F32), pltpu.VMEM((1, w), F32)],
        compiler_params=_params(2),
        name="rglru_mixer",
    )(z, w_d, conv_w, conv_b.reshape(1, w), wax, b_a.reshape(1, w), b_x.reshape(1, w),
      lam.reshape(1, w))


MERGE_COLS = 256


def _merge_kernel(final, z_ref, h_ref, ya_ref, yb_ref, yc_ref, yd_ref, wmg_ref, wbr_ref,
                  wout_ref, nw_ref, *rest):
    if final:
        out_ref, merged_ref = rest
    else:
        hn_ref, zn_ref, merged_ref = rest
    zt = z_ref[0]
    ys = [ya_ref[0], yb_ref[0], yc_ref[0], yd_ref[0]]
    for n in range(D_MODEL // MERGE_COLS):
        cols = slice(n * MERGE_COLS, (n + 1) * MERGE_COLS)
        merged = None
        for b in range(N_BRANCH):
            gcols = slice(b * D_MODEL + n * MERGE_COLS, b * D_MODEL + (n + 1) * MERGE_COLS)
            gate = _sigmoid(_dot(zt, wmg_ref[:, gcols]))
            term = gate * _dot(ys[b], wbr_ref[b, :, cols])
            merged = term if merged is None else merged + term
        merged_ref[:, cols] = merged.astype(BF16)
    out = h_ref[0] + _dot(merged_ref[...], wout_ref[...])
    normed = _rms_norm_rows(out, nw_ref[...])
    if final:
        out_ref[0] = normed
    else:
        hn_ref[0] = out
        zn_ref[0] = normed.astype(BF16)


def _merge(z, h, ys, w_mg, w_br, w_out, next_norm_w, final, ts):
    bsz, seqlen, d = z.shape
    f32_out = jax.ShapeDtypeStruct((bsz, seqlen, d), F32)
    if final:
        out_shape, out_specs = f32_out, _row_spec(ts, d)
    else:
        out_shape = (f32_out, jax.ShapeDtypeStruct((bsz, seqlen, d), BF16))
        out_specs = (_row_spec(ts, d), _row_spec(ts, d))
    return pl.pallas_call(
        functools.partial(_merge_kernel, final),
        grid=(bsz, seqlen // ts),
        in_specs=[_row_spec(ts, d), _row_spec(ts, d)] + [_row_spec(ts, BRANCH_W)] * N_BRANCH
                 + [_full_spec(w_mg.shape), _full_spec(w_br.shape), _full_spec(w_out.shape),
                    _full_spec((1, d))],
        out_specs=out_specs,
        out_shape=out_shape,
        scratch_shapes=[pltpu.VMEM((ts, d), BF16)],
        compiler_params=_params(2),
        name="merge_final" if final else "merge",
    )(z, h, *ys, w_mg, w_br, w_out, next_norm_w.reshape(1, d))


TS_NORM = 1024
TS_MIXER = 512
TS_MERGE = 256


def kernel(x, norm_w, w_in, hg_lb_logits, hg_norm_w, gla_gk_w, gla_gk_b, gla_norm_w, sg_ln_w, sg_ln_b, sg_w, sg_b, lru_conv_w, lru_conv_b, lru_w_a, lru_b_a, lru_w_x, lru_b_x, lru_lambda, w_branch, w_out, final_norm_w):
    dmat = jnp.asarray(_decay_matrix(), BF16)
    masks = jnp.asarray(_level_masks(), F32)
    lr_pad = jnp.zeros((D_MODEL, LANES - GLA_RANK), BF16)
    gkw_pad = jnp.zeros((LANES - GLA_RANK, GLA_HEADS * GLA_DK), BF16)

    z = _first_norm(x, norm_w[0], TS_NORM)
    h = x
    for l in range(DEPTH):
        w16 = w_in[l].astype(BF16)
        w_a = w16[:, OFF_A:OFF_B]
        w_b = jnp.concatenate([w16[:, OFF_B:OFF_LR], w16[:, OFF_LR:OFF_C], lr_pad], axis=1)
        w_c = w16[:, OFF_C:OFF_D]
        w_d = w16[:, OFF_D:OFF_MG]
        w_mg = w16[:, OFF_MG:N_IN]
        gkw = jnp.concatenate([gla_gk_w[l].astype(BF16), gkw_pad], axis=0)
        wax = jnp.concatenate([lru_w_a[l], lru_w_x[l]], axis=-1).astype(BF16)
        sg_bias = jnp.repeat(sg_b[l].T, LANES, axis=1)

        y_a = _hgrn2(z, w_a, hg_lb_logits, hg_norm_w[l], dmat, masks, l, TS_MIXER)
        y_b = _gla(z, w_b, gkw, gla_gk_b[l], gla_norm_w[l], dmat, masks, TS_MIXER)
        y_c = _spatial_gating(z, w_c, sg_ln_w[l], sg_ln_b[l], sg_w[l], sg_bias, TS_MIXER)
        y_d = _rglru(z, w_d, lru_conv_w[l], lru_conv_b[l], wax, lru_b_a[l], lru_b_x[l],
                     lru_lambda[l], TS_MIXER)
        ys = (y_a, y_b, y_c, y_d)
        w_br = w_branch[l].astype(BF16)
        w_o = w_out[l].astype(BF16)
        if l + 1 < DEPTH:
            h, z = _merge(z, h, ys, w_mg, w_br, w_o, norm_w[l + 1], False, TS_MERGE)
        else:
            return _merge(z, h, ys, w_mg, w_br, w_o, final_norm_w, True, TS_MERGE)
```

```python
import functools

import numpy as np
import jax
import jax.numpy as jnp
from jax import lax
from jax.experimental import pallas as pl
from jax.experimental.pallas import tpu as pltpu

F32 = jnp.float32
BF16 = jnp.bfloat16

D_MODEL = 1024
DEPTH = 2
CHUNK = 64
BRANCH_W = 512
N_BRANCH = 4
NORM_EPS = 1e-6
SQRT_EPS = 1e-12
HG_HEADS = 4
HG_DK = 128
GLA_HEADS = 4
GLA_DK = 64
GLA_RANK = 16
GLA_TAU = 16.0
SG_GROUPS = 4
SG_LEN = 128
LRU_HEADS = 4
LRU_HD = 128
CONV_WIDTH = 4
RG_C = 8.0

LANES = 128
SUBLANES = 8
VMEM_LIMIT_BYTES = 56 * 1024 * 1024
LOG2E = 1.4426950408889634

OFF_A = 0
OFF_B = 2048
OFF_LR = 3584
OFF_C = 3600
OFF_D = 5136
OFF_MG = 6160
N_IN = 10256

LEVELS = (32, 16, 8, 4, 2, 1)
N_DECAY_BLOCKS = 2 + len(LEVELS)


def _decay_matrix():
    c = CHUNK
    i = np.arange(c)[:, None]
    j = np.arange(c)[None, :]
    blocks = [(j <= i), (j > i)]
    for s in LEVELS:
        ref = (i // (2 * s)) * (2 * s) + s - 1
        in_b = i > ref
        blocks.append(np.where(in_b, (j > ref) & (j <= i), (j > i) & (j <= ref)))
    m = np.concatenate(blocks, axis=0).astype(np.float32)
    return np.concatenate([m, m, m], axis=1)


def _level_masks():
    c = CHUNK
    i = np.arange(c)[:, None]
    j = np.arange(c)[None, :]
    masks = [(i == j)]
    for s in LEVELS:
        same = (i // (2 * s)) == (j // (2 * s))
        masks.append(same & ((i // s) % 2 == 1) & ((j // s) % 2 == 0))
    return np.stack(masks).astype(np.float32)


def _sigmoid(x):
    return 0.5 + 0.5 * jnp.tanh(0.5 * x)


def _silu(x):
    return x * _sigmoid(x)


def _dot(a, b):
    return jnp.dot(a, b, preferred_element_type=F32)


def _dot_nt(a, b):
    return lax.dot_general(a, b, (((1,), (1,)), ((), ())), preferred_element_type=F32)


def _dot_tn(a, b):
    return lax.dot_general(a, b, (((0,), (0,)), ((), ())), preferred_element_type=F32)


def _split3(x):
    hi = x.astype(BF16)
    r1 = x - hi.astype(F32)
    mid = r1.astype(BF16)
    lo = (r1 - mid.astype(F32)).astype(BF16)
    return jnp.concatenate([hi, mid, lo], axis=0)


def _rms_norm_rows(x, w):
    return x * lax.rsqrt(jnp.mean(x * x, axis=-1, keepdims=True) + NORM_EPS) * w


def _params(n_grid_dims):
    return pltpu.CompilerParams(
        dimension_semantics=("arbitrary",) * n_grid_dims,
        vmem_limit_bytes=VMEM_LIMIT_BYTES,
    )


def _full_spec(shape):
    return pl.BlockSpec(shape, lambda *_: (0,) * len(shape))


def _row_spec(ts, width):
    return pl.BlockSpec((1, ts, width), lambda b, s: (b, s, 0))


def _time_major_spec(ts, width):
    return pl.BlockSpec((ts, width), lambda b, s: (s, b))


def _time_major_shape(bsz, seqlen, width, dtype):
    return jax.ShapeDtypeStruct((seqlen, bsz * width), dtype)


def _norm_kernel(x_ref, w_ref, z_ref):
    z_ref[...] = _rms_norm_rows(x_ref[0], w_ref[...]).astype(BF16)


def _first_norm(x, w, ts):
    bsz, seqlen, d = x.shape
    return pl.pallas_call(
        _norm_kernel,
        grid=(bsz, seqlen // ts),
        in_specs=[_row_spec(ts, d), _full_spec((1, d))],
        out_specs=_time_major_spec(ts, d),
        out_shape=_time_major_shape(bsz, seqlen, d, BF16),
        compiler_params=_params(2),
        name="first_norm",
    )(x, w.reshape(1, d))


def _pipelined_chunks(n_chunks, decay_stage, score_stage):
    assert n_chunks % 2 == 0 and n_chunks >= 2
    decay_stage(0, 0)

    def body(i, carry):
        ci = 2 * i
        decay_stage(ci + 1, 1)
        score_stage(ci, 0)
        decay_stage(ci + 2, 0)
        score_stage(ci + 1, 1)
        return carry

    lax.fori_loop(0, n_chunks // 2 - 1, body, 0)
    decay_stage(n_chunks - 1, 1)
    score_stage(n_chunks - 2, 0)
    score_stage(n_chunks - 1, 1)


def _chunk_rows(ci):
    start = ci * CHUNK
    if not isinstance(ci, int):
        start = pl.multiple_of(start, CHUNK)
    return pl.ds(start, CHUNK)


def _decay_exponentials(log_decay, dmat_ref, e_ref):
    e_ref[...] = jnp.exp2(_dot(dmat_ref[...], _split3(log_decay * LOG2E)))


def _chunk_scores(q_units, k_units, v_heads, heads_per_unit, masks_ref, e_ref, state_ref):
    c = CHUNK
    n_heads = len(v_heads)
    lane = lax.broadcasted_iota(jnp.int32, (c, LANES), 1)
    head_w = LANES // heads_per_unit
    outs = []
    for h in range(n_heads):
        u = h // heads_per_unit
        lanes = slice(u * LANES, (u + 1) * LANES)
        q = q_units[u]
        k = k_units[u]
        if heads_per_unit > 1:
            sub = h % heads_per_unit
            in_head = (lane >= sub * head_w) & (lane < (sub + 1) * head_w)
            q = jnp.where(in_head, q, 0.0)
            k_own = jnp.where(in_head, k, 0.0)
        else:
            k_own = k
        v16 = v_heads[h].astype(BF16)
        scores = masks_ref[0] * _dot_nt(q.astype(BF16), k.astype(BF16))
        for lvl in range(len(LEVELS)):
            e = e_ref[(2 + lvl) * c:(3 + lvl) * c, lanes]
            scores = scores + masks_ref[1 + lvl] * _dot_nt((q * e).astype(BF16), (k * e).astype(BF16))
        e_in = e_ref[0:c, lanes]
        e_out = e_ref[c:2 * c, lanes]
        e_last = e_ref[c - 1:c, lanes]
        st = state_ref[h]
        o = _dot(scores.astype(BF16), v16) + _dot_nt((q * e_in).astype(BF16), st.astype(BF16))
        state_ref[h] = st * e_last + _dot_tn(v16, (k_own * e_out).astype(BF16))
        outs.append(o)
    return outs


def _head_norm_gate(o, norm_w, gate):
    y = o * lax.rsqrt(jnp.mean(o * o, axis=-1, keepdims=True) + NORM_EPS) * norm_w
    return y * _silu(gate)


def _hgrn2_kernel(layer, z_ref, w_ref, lbl_ref, nw_ref, dmat_ref, masks_ref, y_ref,
                  proj_ref, e_ref, key_ref, state_ref):
    @pl.when(pl.program_id(1) == 0)
    def _():
        state_ref[...] = jnp.zeros_like(state_ref)

    proj_ref[...] = _dot(z_ref[...], w_ref[...])

    logits = [lbl_ref[i:i + 1, :] for i in range(DEPTH)]
    mx = functools.reduce(jnp.maximum, logits)
    ex = [jnp.exp(t - mx) for t in logits]
    tot = functools.reduce(lambda a, b: a + b, ex)
    lb = functools.reduce(lambda a, b: a + b, ex[:layer + 1]) / tot - ex[0] / tot
    nw = nw_ref[...]
    w = BRANCH_W
    ts = proj_ref.shape[0]
    units = [slice(h * LANES, (h + 1) * LANES) for h in range(HG_HEADS)]

    def decay_stage(ci, slot):
        rows = _chunk_rows(ci)
        zf = proj_ref[rows, w:2 * w]
        en = jnp.exp(-jnp.abs(zf))
        r = 1.0 / (1.0 + en)
        er = en * r
        pos = zf >= 0.0
        sig = jnp.where(pos, r, er)
        nsig = jnp.where(pos, er, r)
        key_ref[slot] = (1.0 - lb) * nsig
        _decay_exponentials(jnp.log(lb + (1.0 - lb) * sig), dmat_ref, e_ref.at[slot])

    def score_stage(ci, slot):
        rows = _chunk_rows(ci)
        q = proj_ref[rows, 0:w] * (HG_DK ** -0.5)
        key = key_ref[slot]
        vin = proj_ref[rows, 2 * w:3 * w]
        gate = proj_ref[rows, 3 * w:4 * w]
        outs = _chunk_scores([q[:, u] for u in units], [key[:, u] for u in units],
                             [vin[:, u] for u in units], 1, masks_ref, e_ref.at[slot], state_ref)
        for h, u in enumerate(units):
            y_ref[0, rows, u] = _head_norm_gate(outs[h], nw, gate[:, u]).astype(BF16)

    _pipelined_chunks(ts // CHUNK, decay_stage, score_stage)


def _hgrn2(z, w_a, lb_logits, norm_w, dmat, masks, layer, bsz, ts):
    seqlen = z.shape[0]
    d, n_cols = w_a.shape
    return pl.pallas_call(
        functools.partial(_hgrn2_kernel, layer),
        grid=(bsz, seqlen // ts),
        in_specs=[_time_major_spec(ts, d), _full_spec((d, n_cols)), _full_spec(lb_logits.shape),
                  _full_spec((1, LANES)), _full_spec(dmat.shape), _full_spec(masks.shape)],
        out_specs=_row_spec(ts, BRANCH_W),
        out_shape=jax.ShapeDtypeStruct((bsz, seqlen, BRANCH_W), BF16),
        scratch_shapes=[pltpu.VMEM((ts, n_cols), F32),
                        pltpu.VMEM((2, N_DECAY_BLOCKS * CHUNK, BRANCH_W), F32),
                        pltpu.VMEM((2, CHUNK, BRANCH_W), F32),
                        pltpu.VMEM((HG_HEADS, LANES, LANES), F32)],
        compiler_params=_params(2),
        name="hgrn2_mixer",
    )(z, w_a, lb_logits, norm_w.reshape(1, LANES), dmat, masks)


def _gla_kernel(z_ref, w_ref, gkw_ref, gkb_ref, nw_ref, dmat_ref, masks_ref, y_ref,
                proj_ref, e_ref, state_ref):
    @pl.when(pl.program_id(1) == 0)
    def _():
        state_ref[...] = jnp.zeros_like(state_ref)

    proj_ref[...] = _dot(z_ref[...], w_ref[...])
    nw = nw_ref[...]
    gkb = gkb_ref[...]
    qk_w = GLA_HEADS * GLA_DK
    w = BRANCH_W
    ts = proj_ref.shape[0]
    units = [slice(u * LANES, (u + 1) * LANES) for u in range(qk_w // LANES)]
    heads = [slice(h * LANES, (h + 1) * LANES) for h in range(GLA_HEADS)]

    def decay_stage(ci, slot):
        rows = _chunk_rows(ci)
        lowrank = proj_ref[rows, 2 * qk_w + 2 * w:2 * qk_w + 2 * w + LANES]
        gk = _dot(lowrank.astype(BF16), gkw_ref[...]) + gkb
        log_alpha = (jnp.minimum(gk, 0.0) - jnp.log(1.0 + jnp.exp(-jnp.abs(gk)))) * (1.0 / GLA_TAU)
        _decay_exponentials(log_alpha, dmat_ref, e_ref.at[slot])

    def score_stage(ci, slot):
        rows = _chunk_rows(ci)
        q = proj_ref[rows, 0:qk_w] * (GLA_DK ** -0.5)
        k = proj_ref[rows, qk_w:2 * qk_w]
        v = proj_ref[rows, 2 * qk_w:2 * qk_w + w]
        gate = proj_ref[rows, 2 * qk_w + w:2 * qk_w + 2 * w]
        outs = _chunk_scores([q[:, u] for u in units], [k[:, u] for u in units],
                             [v[:, hs] for hs in heads], LANES // GLA_DK, masks_ref,
                             e_ref.at[slot], state_ref)
        for h, hs in enumerate(heads):
            y_ref[0, rows, hs] = _head_norm_gate(outs[h], nw, gate[:, hs]).astype(BF16)

    _pipelined_chunks(ts // CHUNK, decay_stage, score_stage)


def _gla(z, w_b, gk_w, gk_b, norm_w, dmat, masks, bsz, ts):
    seqlen = z.shape[0]
    d, n_cols = w_b.shape
    qk_w = GLA_HEADS * GLA_DK
    return pl.pallas_call(
        _gla_kernel,
        grid=(bsz, seqlen // ts),
        in_specs=[_time_major_spec(ts, d), _full_spec((d, n_cols)), _full_spec((LANES, qk_w)),
                  _full_spec((1, qk_w)), _full_spec((1, LANES)), _full_spec(dmat.shape),
                  _full_spec(masks.shape)],
        out_specs=_row_spec(ts, BRANCH_W),
        out_shape=jax.ShapeDtypeStruct((bsz, seqlen, BRANCH_W), BF16),
        scratch_shapes=[pltpu.VMEM((ts, n_cols), F32),
                        pltpu.VMEM((2, N_DECAY_BLOCKS * CHUNK, qk_w), F32),
                        pltpu.VMEM((GLA_HEADS, LANES, LANES), F32)],
        compiler_params=_params(2),
        name="gla_mixer",
    )(z, w_b, gk_w, gk_b.reshape(1, qk_w), norm_w.reshape(1, LANES), dmat, masks)


def _gelu(x):
    return 0.5 * x * (1.0 + lax.erf(x * (2.0 ** -0.5)))


def _sg_kernel(z_ref, w_ref, lnw_ref, lnb_ref, ws_ref, bias_ref, y_ref):
    w = BRANCH_W
    proj = _dot(z_ref[...], w_ref[...])
    ts = proj.shape[0]
    u = _gelu(proj[:, 0:w])
    vf = _gelu(proj[:, w:2 * w])
    gate = proj[:, 2 * w:3 * w]
    mu = jnp.mean(vf, axis=-1, keepdims=True)
    dv = vf - mu
    var = jnp.mean(dv * dv, axis=-1, keepdims=True)
    vn = (dv * lax.rsqrt(var + NORM_EPS) * lnw_ref[...] + lnb_ref[...]).astype(BF16)
    ri = lax.broadcasted_iota(jnp.int32, (SG_LEN, SG_LEN), 0)
    cj = lax.broadcasted_iota(jnp.int32, (SG_LEN, SG_LEN), 1)
    allowed = (cj < CHUNK) | (ri >= CHUNK)
    wm = [jnp.where(allowed, ws_ref[g], 0.0).astype(BF16) for g in range(SG_GROUPS)]
    bias = bias_ref[...]
    ug = u * _silu(gate)
    for blk in range(ts // SG_LEN):
        rows = slice(blk * SG_LEN, (blk + 1) * SG_LEN)
        for g in range(SG_GROUPS):
            cols = slice(g * LANES, (g + 1) * LANES)
            mixed = _dot(wm[g], vn[rows, cols]) + bias[:, cols]
            y_ref[0, rows, cols] = (ug[rows, cols] * mixed).astype(BF16)


def _spatial_gating(z, w_c, ln_w, ln_b, w_s, bias, bsz, ts):
    seqlen = z.shape[0]
    d, n_cols = w_c.shape
    return pl.pallas_call(
        _sg_kernel,
        grid=(bsz, seqlen // ts),
        in_specs=[_time_major_spec(ts, d), _full_spec((d, n_cols)), _full_spec((1, BRANCH_W)),
                  _full_spec((1, BRANCH_W)), _full_spec(w_s.shape), _full_spec(bias.shape)],
        out_specs=_row_spec(ts, BRANCH_W),
        out_shape=jax.ShapeDtypeStruct((bsz, seqlen, BRANCH_W), BF16),
        compiler_params=_params(2),
        name="spatial_gating_mixer",
    )(z, w_c, ln_w.reshape(1, BRANCH_W), ln_b.reshape(1, BRANCH_W), w_s, bias)


def _expm1_nonpos(x):
    series = x * (1.0 + x * (1.0 / 2) * (1.0 + x * (1.0 / 3) * (1.0 + x * (1.0 / 4) * (
        1.0 + x * (1.0 / 5) * (1.0 + x * (1.0 / 6) * (1.0 + x * (1.0 / 7)))))))
    return jnp.where(x > -0.25, series, jnp.exp(x) - 1.0)


def _lru_kernel(bsz, z_ref, w_ref, cw_ref, cb_ref, wax_ref, ba_ref, bx_ref, lam_ref, y_ref,
                xext_ref, hprev_ref):
    w = BRANCH_W
    tail = (CONV_WIDTH - 1) * bsz

    @pl.when(pl.program_id(0) == 0)
    def _():
        xext_ref[0:tail, :] = jnp.zeros((tail, w), F32)
        hprev_ref[...] = jnp.zeros_like(hprev_ref)

    proj = _dot(z_ref[...], w_ref[...])
    n_rows = proj.shape[0]
    xr = proj[:, 0:w]
    gate = proj[:, w:2 * w]
    xext_ref[tail:tail + n_rows, :] = xr
    xc = cb_ref[...]
    for tap in range(CONV_WIDTH):
        back = (CONV_WIDTH - 1 - tap) * bsz
        xs = xr if back == 0 else xext_ref[tail - back:tail - back + n_rows, :]
        xc = xc + xs * cw_ref[tap:tap + 1, :]
    xext_ref[0:tail, :] = xr[n_rows - tail:n_rows, :]

    xc16 = xc.astype(BF16)
    ra, ix = [], []
    for h in range(LRU_HEADS):
        cols = slice(h * LRU_HD, (h + 1) * LRU_HD)
        both = _dot(xc16[:, cols], wax_ref[h])
        ra.append(both[:, 0:LRU_HD])
        ix.append(both[:, LRU_HD:2 * LRU_HD])
    r = _sigmoid(jnp.concatenate(ra, axis=1) + ba_ref[...])
    ig = _sigmoid(jnp.concatenate(ix, axis=1) + bx_ref[...])
    nlam = -lam_ref[...]
    softplus = jnp.maximum(nlam, 0.0) + jnp.log(1.0 + jnp.exp(-jnp.abs(nlam)))
    log_a = (-RG_C) * r * softplus
    a = jnp.exp(log_a)
    bt = jnp.sqrt(jnp.maximum(-_expm1_nonpos(2.0 * log_a), SQRT_EPS)) * (ig * xc)

    hcur = hprev_ref[...]
    hs = []
    for t in range(n_rows // bsz):
        rows = slice(t * bsz, (t + 1) * bsz)
        hcur = a[rows, :] * hcur + bt[rows, :]
        hs.append(hcur)
    hprev_ref[...] = hcur
    y_ref[...] = (jnp.concatenate(hs, axis=0) * _silu(gate)).astype(BF16)


def _rglru(z, w_d, conv_w, conv_b, wax, b_a, b_x, lam, bsz, tt):
    seqlen = z.shape[0]
    d, n_cols = w_d.shape
    w = BRANCH_W
    assert bsz == SUBLANES, "RG-LRU kernel puts the batch on the vreg sublanes"
    n_rows = tt * bsz
    return pl.pallas_call(
        functools.partial(_lru_kernel, bsz),
        grid=(seqlen // tt,),
        in_specs=[pl.BlockSpec((n_rows, d), lambda i: (i, 0)), _full_spec((d, n_cols)),
                  _full_spec((CONV_WIDTH, w)), _full_spec((1, w)), _full_spec(wax.shape),
                  _full_spec((1, w)), _full_spec((1, w)), _full_spec((1, w))],
        out_specs=pl.BlockSpec((n_rows, w), lambda i: (i, 0)),
        out_shape=jax.ShapeDtypeStruct((seqlen * bsz, w), BF16),
        scratch_shapes=[pltpu.VMEM((n_rows + (CONV_WIDTH - 1) * bsz, w), F32),
                        pltpu.VMEM((bsz, w), F32)],
        compiler_params=_params(1),
        name="rglru_mixer",
    )(z.reshape(seqlen * bsz, d), w_d, conv_w, conv_b.reshape(1, w), wax, b_a.reshape(1, w),
      b_x.reshape(1, w), lam.reshape(1, w))


MERGE_COLS = 256


def _merge_kernel(final, z_ref, h_ref, ya_ref, yb_ref, yc_ref, yd_ref, wmg_ref, wbr_ref,
                  wout_ref, nw_ref, *rest):
    if final:
        out_ref, merged_ref = rest
    else:
        hn_ref, zn_ref, merged_ref = rest
    zt = z_ref[...]
    ys = [ya_ref[0], yb_ref[0], yc_ref[0], yd_ref[...]]
    for n in range(D_MODEL // MERGE_COLS):
        cols = slice(n * MERGE_COLS, (n + 1) * MERGE_COLS)
        merged = None
        for b in range(N_BRANCH):
            gcols = slice(b * D_MODEL + n * MERGE_COLS, b * D_MODEL + (n + 1) * MERGE_COLS)
            gate = _sigmoid(_dot(zt, wmg_ref[:, gcols]))
            term = gate * _dot(ys[b], wbr_ref[b, :, cols])
            merged = term if merged is None else merged + term
        merged_ref[:, cols] = merged.astype(BF16)
    out = h_ref[0] + _dot(merged_ref[...], wout_ref[...])
    normed = _rms_norm_rows(out, nw_ref[...])
    if final:
        out_ref[0] = normed
    else:
        hn_ref[0] = out
        zn_ref[...] = normed.astype(BF16)


def _merge(z, h, ys, w_mg, w_br, w_out, next_norm_w, final, ts):
    bsz, seqlen, d = h.shape
    f32_out = jax.ShapeDtypeStruct((bsz, seqlen, d), F32)
    if final:
        out_shape, out_specs = f32_out, _row_spec(ts, d)
    else:
        out_shape = (f32_out, _time_major_shape(bsz, seqlen, d, BF16))
        out_specs = (_row_spec(ts, d), _time_major_spec(ts, d))
    y_d = ys[3].reshape(seqlen, bsz * BRANCH_W)
    return pl.pallas_call(
        functools.partial(_merge_kernel, final),
        grid=(bsz, seqlen // ts),
        in_specs=[_time_major_spec(ts, d), _row_spec(ts, d)] + [_row_spec(ts, BRANCH_W)] * 3
                 + [_time_major_spec(ts, BRANCH_W), _full_spec(w_mg.shape),
                    _full_spec(w_br.shape), _full_spec(w_out.shape), _full_spec((1, d))],
        out_specs=out_specs,
        out_shape=out_shape,
        scratch_shapes=[pltpu.VMEM((ts, d), BF16)],
        compiler_params=_params(2),
        name="merge_final" if final else "merge",
    )(z, h, ys[0], ys[1], ys[2], y_d, w_mg, w_br, w_out, next_norm_w.reshape(1, d))


TS_NORM = 1024
TS_ATTN = 1024
TS_SG = 512
TT_LRU = 64
TS_MERGE = 256


def kernel(x, norm_w, w_in, hg_lb_logits, hg_norm_w, gla_gk_w, gla_gk_b, gla_norm_w, sg_ln_w, sg_ln_b, sg_w, sg_b, lru_conv_w, lru_conv_b, lru_w_a, lru_b_a, lru_w_x, lru_b_x, lru_lambda, w_branch, w_out, final_norm_w):
    bsz = x.shape[0]
    dmat = jnp.asarray(_decay_matrix(), BF16)
    masks = jnp.asarray(_level_masks(), F32)
    lr_pad = jnp.zeros((D_MODEL, LANES - GLA_RANK), BF16)
    gkw_pad = jnp.zeros((LANES - GLA_RANK, GLA_HEADS * GLA_DK), BF16)

    z = _first_norm(x, norm_w[0], TS_NORM)
    h = x
    for l in range(DEPTH):
        w16 = w_in[l].astype(BF16)
        w_a = w16[:, OFF_A:OFF_B]
        w_b = jnp.concatenate([w16[:, OFF_B:OFF_LR], w16[:, OFF_LR:OFF_C], lr_pad], axis=1)
        w_c = w16[:, OFF_C:OFF_D]
        w_d = w16[:, OFF_D:OFF_MG]
        w_mg = w16[:, OFF_MG:N_IN]
        gkw = jnp.concatenate([gla_gk_w[l].astype(BF16), gkw_pad], axis=0)
        wax = jnp.concatenate([lru_w_a[l], lru_w_x[l]], axis=-1).astype(BF16)
        sg_bias = jnp.repeat(sg_b[l].T, LANES, axis=1)

        y_a = _hgrn2(z, w_a, hg_lb_logits, hg_norm_w[l], dmat, masks, l, bsz, TS_ATTN)
        y_b = _gla(z, w_b, gkw, gla_gk_b[l], gla_norm_w[l], dmat, masks, bsz, TS_ATTN)
        y_c = _spatial_gating(z, w_c, sg_ln_w[l], sg_ln_b[l], sg_w[l], sg_bias, bsz, TS_SG)
        y_d = _rglru(z, w_d, lru_conv_w[l], lru_conv_b[l], wax, lru_b_a[l], lru_b_x[l],
                     lru_lambda[l], bsz, TT_LRU)
        ys = (y_a, y_b, y_c, y_d)
        w_br = w_branch[l].astype(BF16)
        w_o = w_out[l].astype(BF16)
        if l + 1 < DEPTH:
            h, z = _merge(z, h, ys, w_mg, w_br, w_o, norm_w[l + 1], False, TS_MERGE)
        else:
            return _merge(z, h, ys, w_mg, w_br, w_o, final_norm_w, True, TS_MERGE)
```

```python
import functools

import numpy as np
import jax
import jax.numpy as jnp
from jax import lax
from jax.experimental import pallas as pl
from jax.experimental.pallas import tpu as pltpu

F32 = jnp.float32
BF16 = jnp.bfloat16

D_MODEL = 1024
DEPTH = 2
CHUNK = 64
BRANCH_W = 512
N_BRANCH = 4
NORM_EPS = 1e-6
SQRT_EPS = 1e-12
HG_HEADS = 4
HG_DK = 128
GLA_HEADS = 4
GLA_DK = 64
GLA_RANK = 16
GLA_TAU = 16.0
SG_GROUPS = 4
SG_LEN = 128
LRU_HEADS = 4
LRU_HD = 128
CONV_WIDTH = 4
RG_C = 8.0

LANES = 128
SUBLANES = 8
VMEM_LIMIT_BYTES = 56 * 1024 * 1024
LOG2E = 1.4426950408889634

OFF_A = 0
OFF_B = 2048
OFF_LR = 3584
OFF_C = 3600
OFF_D = 5136
OFF_MG = 6160
N_IN = 10256

LEVELS = (32, 16, 8, 4, 2, 1)
N_DECAY_BLOCKS = 2 + len(LEVELS)


def _decay_matrix():
    c = CHUNK
    i = np.arange(c)[:, None]
    j = np.arange(c)[None, :]
    blocks = [(j <= i), (j > i)]
    for s in LEVELS:
        ref = (i // (2 * s)) * (2 * s) + s - 1
        in_b = i > ref
        blocks.append(np.where(in_b, (j > ref) & (j <= i), (j > i) & (j <= ref)))
    m = np.concatenate(blocks, axis=0).astype(np.float32)
    return np.concatenate([m, m, m], axis=1)


def _level_masks():
    c = CHUNK
    i = np.arange(c)[:, None]
    j = np.arange(c)[None, :]
    masks = [(i == j)]
    for s in LEVELS:
        same = (i // (2 * s)) == (j // (2 * s))
        masks.append(same & ((i // s) % 2 == 1) & ((j // s) % 2 == 0))
    return np.stack(masks).astype(np.float32)


def _sigmoid(x):
    return 0.5 + 0.5 * jnp.tanh(0.5 * x)


def _silu(x):
    half = 0.5 * x
    return half + half * jnp.tanh(half)


def _dot(a, b):
    return jnp.dot(a, b, preferred_element_type=F32)


def _dot_nt(a, b):
    return lax.dot_general(a, b, (((1,), (1,)), ((), ())), preferred_element_type=F32)


def _dot_tn(a, b):
    return lax.dot_general(a, b, (((0,), (0,)), ((), ())), preferred_element_type=F32)


def _split3(x):
    hi = x.astype(BF16)
    r1 = x - hi.astype(F32)
    mid = r1.astype(BF16)
    lo = (r1 - mid.astype(F32)).astype(BF16)
    return jnp.concatenate([hi, mid, lo], axis=0)


def _rms_norm_rows(x, w):
    return x * lax.rsqrt(jnp.mean(x * x, axis=-1, keepdims=True) + NORM_EPS) * w


def _params(n_grid_dims, flags=None):
    return pltpu.CompilerParams(
        dimension_semantics=("arbitrary",) * n_grid_dims,
        vmem_limit_bytes=VMEM_LIMIT_BYTES,
        flags=flags,
    )


def _full_spec(shape):
    return pl.BlockSpec(shape, lambda *_: (0,) * len(shape))


def _row_spec(ts, width):
    return pl.BlockSpec((1, ts, width), lambda b, s: (b, s, 0))


def _norm_kernel(x_ref, w_ref, z_ref):
    z_ref[0] = _rms_norm_rows(x_ref[0], w_ref[...]).astype(BF16)


def _first_norm(x, w, ts):
    bsz, seqlen, d = x.shape
    return pl.pallas_call(
        _norm_kernel,
        grid=(bsz, seqlen // ts),
        in_specs=[_row_spec(ts, d), _full_spec((1, d))],
        out_specs=_row_spec(ts, d),
        out_shape=jax.ShapeDtypeStruct((bsz, seqlen, d), BF16),
        compiler_params=_params(2),
        name="first_norm",
    )(x, w.reshape(1, d))


def _pipelined_chunks(n_chunks, decay_stage, score_stage):
    assert n_chunks % 2 == 0 and n_chunks >= 2
    decay_stage(0, 0)

    def body(i, carry):
        ci = 2 * i
        decay_stage(ci + 1, 1)
        score_stage(ci, 0)
        decay_stage(ci + 2, 0)
        score_stage(ci + 1, 1)
        return carry

    lax.fori_loop(0, n_chunks // 2 - 1, body, 0)
    decay_stage(n_chunks - 1, 1)
    score_stage(n_chunks - 2, 0)
    score_stage(n_chunks - 1, 1)


def _chunk_rows(ci):
    start = ci * CHUNK
    if not isinstance(ci, int):
        start = pl.multiple_of(start, CHUNK)
    return pl.ds(start, CHUNK)


def _decay_exponentials(log_decay, dmat_ref, e_ref, elast_ref):
    ex = jnp.exp2(_dot(dmat_ref[...], _split3(log_decay * LOG2E)))
    e_ref[...] = ex.astype(BF16)
    elast_ref[...] = ex[CHUNK - SUBLANES:CHUNK, :]


def _chunk_scores(q_units, k_units, v_heads, heads_per_unit, masks_ref, e_ref, elast_ref,
                  state_ref):
    c = CHUNK
    n_heads = len(v_heads)
    lane = lax.broadcasted_iota(jnp.int32, (c, LANES), 1)
    head_w = LANES // heads_per_unit
    unit_lanes = [slice(u * LANES, (u + 1) * LANES) for u in range(len(q_units))]
    k16_units = [k.astype(BF16) for k in k_units]
    q16_heads, k16_heads, kown_heads = [], [], []
    for h in range(n_heads):
        u = h // heads_per_unit
        q16 = q_units[u].astype(BF16)
        k16 = k16_units[u]
        if heads_per_unit > 1:
            sub = h % heads_per_unit
            in_head = (lane >= sub * head_w) & (lane < (sub + 1) * head_w)
            q16 = jnp.where(in_head, q16, jnp.zeros_like(q16))
            kown_heads.append(jnp.where(in_head, k16, jnp.zeros_like(k16)))
        else:
            kown_heads.append(k16)
        q16_heads.append(q16)
        k16_heads.append(k16)

    scores = [masks_ref[0] * _dot_nt(q16_heads[h], k16_heads[h]) for h in range(n_heads)]
    for lvl in range(len(LEVELS)):
        e_units = [e_ref[(2 + lvl) * c:(3 + lvl) * c, ul] for ul in unit_lanes]
        ke_units = [k16_units[u] * e_units[u] for u in range(len(unit_lanes))]
        for h in range(n_heads):
            u = h // heads_per_unit
            scores[h] = scores[h] + masks_ref[1 + lvl] * _dot_nt(q16_heads[h] * e_units[u], ke_units[u])

    v16_heads = [v.astype(BF16) for v in v_heads]
    states = [state_ref[h] for h in range(n_heads)]
    outs = []
    for h in range(n_heads):
        e_in = e_ref[0:c, unit_lanes[h // heads_per_unit]]
        outs.append(_dot(scores[h].astype(BF16), v16_heads[h])
                    + _dot_nt(q16_heads[h] * e_in, states[h].astype(BF16)))
    for h in range(n_heads):
        lanes = unit_lanes[h // heads_per_unit]
        e_out = e_ref[c:2 * c, lanes]
        e_last = elast_ref[SUBLANES - 1:SUBLANES, lanes]
        state_ref[h] = states[h] * e_last + _dot_tn(v16_heads[h], kown_heads[h] * e_out)
    return outs


def _head_norm_gate(o, norm_w, gate):
    y = o * lax.rsqrt(jnp.mean(o * o, axis=-1, keepdims=True) + NORM_EPS) * norm_w
    return y * _silu(gate)


def _hgrn2_kernel(layer, z_ref, w_ref, lbl_ref, nw_ref, dmat_ref, masks_ref, y_ref,
                  proj_ref, e_ref, elast_ref, key_ref, state_ref):
    @pl.when(pl.program_id(1) == 0)
    def _():
        state_ref[...] = jnp.zeros_like(state_ref)

    proj_ref[...] = _dot(z_ref[0], w_ref[...])

    logits = [lbl_ref[i:i + 1, :] for i in range(DEPTH)]
    mx = functools.reduce(jnp.maximum, logits)
    ex = [jnp.exp(t - mx) for t in logits]
    tot = functools.reduce(lambda a, b: a + b, ex)
    lb = functools.reduce(lambda a, b: a + b, ex[:layer + 1]) / tot - ex[0] / tot
    nw = nw_ref[...]
    w = BRANCH_W
    ts = proj_ref.shape[0]
    units = [slice(h * LANES, (h + 1) * LANES) for h in range(HG_HEADS)]

    def decay_stage(ci, slot):
        rows = _chunk_rows(ci)
        zf = proj_ref[rows, w:2 * w]
        en = jnp.exp(-jnp.abs(zf))
        r = 1.0 / (1.0 + en)
        er = en * r
        pos = zf >= 0.0
        sig = jnp.where(pos, r, er)
        nsig = jnp.where(pos, er, r)
        key_ref[slot] = (1.0 - lb) * nsig
        _decay_exponentials(jnp.log(lb + (1.0 - lb) * sig), dmat_ref, e_ref.at[slot], elast_ref.at[slot])

    def score_stage(ci, slot):
        rows = _chunk_rows(ci)
        q = proj_ref[rows, 0:w] * (HG_DK ** -0.5)
        key = key_ref[slot]
        vin = proj_ref[rows, 2 * w:3 * w]
        gate = proj_ref[rows, 3 * w:4 * w]
        outs = _chunk_scores([q[:, u] for u in units], [key[:, u] for u in units],
                             [vin[:, u] for u in units], 1, masks_ref, e_ref.at[slot], elast_ref.at[slot], state_ref)
        for h, u in enumerate(units):
            y_ref[0, rows, u] = _head_norm_gate(outs[h], nw, gate[:, u]).astype(BF16)

    _pipelined_chunks(ts // CHUNK, decay_stage, score_stage)


def _hgrn2(z, w_a, lb_logits, norm_w, dmat, masks, layer, ts):
    bsz, seqlen, d = z.shape
    n_cols = w_a.shape[1]
    return pl.pallas_call(
        functools.partial(_hgrn2_kernel, layer),
        grid=(bsz, seqlen // ts),
        in_specs=[_row_spec(ts, d), _full_spec((d, n_cols)), _full_spec(lb_logits.shape),
                  _full_spec((1, LANES)), _full_spec(dmat.shape), _full_spec(masks.shape)],
        out_specs=_row_spec(ts, BRANCH_W),
        out_shape=jax.ShapeDtypeStruct((bsz, seqlen, BRANCH_W), BF16),
        scratch_shapes=[pltpu.VMEM((ts, n_cols), F32),
                        pltpu.VMEM((2, N_DECAY_BLOCKS * CHUNK, BRANCH_W), BF16),
                        pltpu.VMEM((2, SUBLANES, BRANCH_W), F32),
                        pltpu.VMEM((2, CHUNK, BRANCH_W), F32),
                        pltpu.VMEM((HG_HEADS, LANES, LANES), F32)],
        compiler_params=_params(2),
        name="hgrn2_mixer",
    )(z, w_a, lb_logits, norm_w.reshape(1, LANES), dmat, masks)


def _gla_kernel(z_ref, w_ref, gkw_ref, gkb_ref, nw_ref, dmat_ref, masks_ref, y_ref,
                proj_ref, e_ref, elast_ref, state_ref):
    @pl.when(pl.program_id(1) == 0)
    def _():
        state_ref[...] = jnp.zeros_like(state_ref)

    proj_ref[...] = _dot(z_ref[0], w_ref[...])
    nw = nw_ref[...]
    gkb = gkb_ref[...]
    qk_w = GLA_HEADS * GLA_DK
    w = BRANCH_W
    ts = proj_ref.shape[0]
    units = [slice(u * LANES, (u + 1) * LANES) for u in range(qk_w // LANES)]
    heads = [slice(h * LANES, (h + 1) * LANES) for h in range(GLA_HEADS)]

    def decay_stage(ci, slot):
        rows = _chunk_rows(ci)
        lowrank = proj_ref[rows, 2 * qk_w + 2 * w:2 * qk_w + 2 * w + LANES]
        gk = _dot(lowrank.astype(BF16), gkw_ref[...]) + gkb
        log_alpha = (jnp.minimum(gk, 0.0) - jnp.log(1.0 + jnp.exp(-jnp.abs(gk)))) * (1.0 / GLA_TAU)
        _decay_exponentials(log_alpha, dmat_ref, e_ref.at[slot], elast_ref.at[slot])

    def score_stage(ci, slot):
        rows = _chunk_rows(ci)
        q = proj_ref[rows, 0:qk_w] * (GLA_DK ** -0.5)
        k = proj_ref[rows, qk_w:2 * qk_w]
        v = proj_ref[rows, 2 * qk_w:2 * qk_w + w]
        gate = proj_ref[rows, 2 * qk_w + w:2 * qk_w + 2 * w]
        outs = _chunk_scores([q[:, u] for u in units], [k[:, u] for u in units],
                             [v[:, hs] for hs in heads], LANES // GLA_DK, masks_ref,
                             e_ref.at[slot], elast_ref.at[slot], state_ref)
        for h, hs in enumerate(heads):
            y_ref[0, rows, hs] = _head_norm_gate(outs[h], nw, gate[:, hs]).astype(BF16)

    _pipelined_chunks(ts // CHUNK, decay_stage, score_stage)


def _gla(z, w_b, gk_w, gk_b, norm_w, dmat, masks, ts):
    bsz, seqlen, d = z.shape
    n_cols = w_b.shape[1]
    qk_w = GLA_HEADS * GLA_DK
    return pl.pallas_call(
        _gla_kernel,
        grid=(bsz, seqlen // ts),
        in_specs=[_row_spec(ts, d), _full_spec((d, n_cols)), _full_spec((LANES, qk_w)),
                  _full_spec((1, qk_w)), _full_spec((1, LANES)), _full_spec(dmat.shape),
                  _full_spec(masks.shape)],
        out_specs=_row_spec(ts, BRANCH_W),
        out_shape=jax.ShapeDtypeStruct((bsz, seqlen, BRANCH_W), BF16),
        scratch_shapes=[pltpu.VMEM((ts, n_cols), F32),
                        pltpu.VMEM((2, N_DECAY_BLOCKS * CHUNK, qk_w), BF16),
                        pltpu.VMEM((2, SUBLANES, qk_w), F32),
                        pltpu.VMEM((GLA_HEADS, LANES, LANES), F32)],
        compiler_params=_params(2),
        name="gla_mixer",
    )(z, w_b, gk_w, gk_b.reshape(1, qk_w), norm_w.reshape(1, LANES), dmat, masks)


def _gelu(x):
    return 0.5 * x * (1.0 + lax.erf(x * (2.0 ** -0.5)))


def _sg_kernel(z_ref, w_ref, lnw_ref, lnb_ref, ws_ref, bias_ref, y_ref):
    w = BRANCH_W
    proj = _dot(z_ref[0], w_ref[...])
    ts = proj.shape[0]
    u = _gelu(proj[:, 0:w])
    vf = _gelu(proj[:, w:2 * w])
    gate = proj[:, 2 * w:3 * w]
    mu = jnp.mean(vf, axis=-1, keepdims=True)
    dv = vf - mu
    var = jnp.mean(dv * dv, axis=-1, keepdims=True)
    vn = (dv * lax.rsqrt(var + NORM_EPS) * lnw_ref[...] + lnb_ref[...]).astype(BF16)
    ri = lax.broadcasted_iota(jnp.int32, (SG_LEN, SG_LEN), 0)
    cj = lax.broadcasted_iota(jnp.int32, (SG_LEN, SG_LEN), 1)
    allowed = (cj < CHUNK) | (ri >= CHUNK)
    wm = [jnp.where(allowed, ws_ref[g], 0.0).astype(BF16) for g in range(SG_GROUPS)]
    bias = bias_ref[...]
    ug = u * _silu(gate)
    for blk in range(ts // SG_LEN):
        rows = slice(blk * SG_LEN, (blk + 1) * SG_LEN)
        for g in range(SG_GROUPS):
            cols = slice(g * LANES, (g + 1) * LANES)
            mixed = _dot(wm[g], vn[rows, cols]) + bias[:, cols]
            y_ref[0, rows, cols] = (ug[rows, cols] * mixed).astype(BF16)


def _spatial_gating(z, w_c, ln_w, ln_b, w_s, bias, ts):
    bsz, seqlen, d = z.shape
    n_cols = w_c.shape[1]
    return pl.pallas_call(
        _sg_kernel,
        grid=(bsz, seqlen // ts),
        in_specs=[_row_spec(ts, d), _full_spec((d, n_cols)), _full_spec((1, BRANCH_W)),
                  _full_spec((1, BRANCH_W)), _full_spec(w_s.shape), _full_spec(bias.shape)],
        out_specs=_row_spec(ts, BRANCH_W),
        out_shape=jax.ShapeDtypeStruct((bsz, seqlen, BRANCH_W), BF16),
        compiler_params=_params(2),
        name="spatial_gating_mixer",
    )(z, w_c, ln_w.reshape(1, BRANCH_W), ln_b.reshape(1, BRANCH_W), w_s, bias)


def _expm1_nonpos(x):
    p = 1.0 / 720
    for coeff in (1.0 / 120, 1.0 / 24, 1.0 / 6, 0.5, 1.0):
        p = p * x + coeff
    return jnp.where(x > -0.125, p * x, jnp.exp(x) - 1.0)


def _lru_kernel(z_ref, w_ref, cw_ref, cb_ref, wax_ref, ba_ref, bx_ref, lam_ref, y_ref,
                xext_ref, ab_ref, hs_ref, hprev_ref):
    bsz, tt, d = z_ref.shape
    w = BRANCH_W
    pad = SUBLANES
    n_rows = bsz * tt
    n_slabs = w // LANES

    @pl.when(pl.program_id(0) == 0)
    def _():
        xext_ref[:, 0:pad, :] = jnp.zeros((bsz, pad, w), F32)
        hprev_ref[...] = jnp.zeros_like(hprev_ref)

    proj = _dot(z_ref[...].reshape(n_rows, d), w_ref[...])
    xr = proj[:, 0:w]
    gate = proj[:, w:2 * w]
    xr3 = xr.reshape(bsz, tt, w)
    xext_ref[:, pad:pad + tt, :] = xr3
    xc = cb_ref[...]
    for tap in range(CONV_WIDTH):
        back = CONV_WIDTH - 1 - tap
        xs = xr if back == 0 else xext_ref[:, pad - back:pad - back + tt, :].reshape(n_rows, w)
        xc = xc + xs * cw_ref[tap:tap + 1, :]
    xext_ref[:, 0:pad, :] = xr3[:, tt - pad:tt, :]

    xc16 = xc.astype(BF16)
    ra, ix = [], []
    for h in range(LRU_HEADS):
        cols = slice(h * LRU_HD, (h + 1) * LRU_HD)
        both = _dot(xc16[:, cols], wax_ref[h])
        ra.append(both[:, 0:LRU_HD])
        ix.append(both[:, LRU_HD:2 * LRU_HD])
    ig = _sigmoid(jnp.concatenate(ix, axis=1) + bx_ref[...])
    nlam = -lam_ref[...]
    softplus = jnp.maximum(nlam, 0.0) + jnp.log(1.0 + jnp.exp(-jnp.abs(nlam)))
    half_scale = (-0.5 * RG_C) * softplus
    log_a = half_scale + half_scale * jnp.tanh(0.5 * (jnp.concatenate(ra, axis=1) + ba_ref[...]))
    a = jnp.exp(log_a)
    var = jnp.maximum(-_expm1_nonpos(2.0 * log_a), SQRT_EPS)
    bt = (var * lax.rsqrt(var)) * (ig * xc)

    for b in range(bsz):
        rows = slice(b * tt, (b + 1) * tt)
        dst = pl.ds(b, tt, stride=bsz)
        for j in range(n_slabs):
            cols = slice(j * LANES, (j + 1) * LANES)
            ab_ref[0, j, dst, :] = a[rows, cols]
            ab_ref[1, j, dst, :] = bt[rows, cols]

    hcur = [hprev_ref[j] for j in range(n_slabs)]
    for t in range(tt):
        step = slice(t * bsz, (t + 1) * bsz)
        for j in range(n_slabs):
            hcur[j] = ab_ref[0, j, step, :] * hcur[j] + ab_ref[1, j, step, :]
            hs_ref[j, step, :] = hcur[j]
    for j in range(n_slabs):
        hprev_ref[j] = hcur[j]

    hseq = jnp.concatenate(
        [jnp.concatenate([hs_ref[j, pl.ds(b, tt, stride=bsz), :] for j in range(n_slabs)], axis=1)
         for b in range(bsz)], axis=0)
    y_ref[...] = (hseq * _silu(gate)).astype(BF16).reshape(bsz, tt, w)


def _rglru(z, w_d, conv_w, conv_b, wax, b_a, b_x, lam, tt):
    bsz, seqlen, d = z.shape
    n_cols = w_d.shape[1]
    w = BRANCH_W
    assert bsz == SUBLANES, "the RG-LRU scan puts the batch on the vreg sublanes"
    n_slabs = w // LANES
    return pl.pallas_call(
        _lru_kernel,
        grid=(seqlen // tt,),
        in_specs=[pl.BlockSpec((bsz, tt, d), lambda i: (0, i, 0)), _full_spec((d, n_cols)),
                  _full_spec((CONV_WIDTH, w)), _full_spec((1, w)), _full_spec(wax.shape),
                  _full_spec((1, w)), _full_spec((1, w)), _full_spec((1, w))],
        out_specs=pl.BlockSpec((bsz, tt, w), lambda i: (0, i, 0)),
        out_shape=jax.ShapeDtypeStruct((bsz, seqlen, w), BF16),
        scratch_shapes=[pltpu.VMEM((bsz, tt + SUBLANES, w), F32),
                        pltpu.VMEM((2, n_slabs, bsz * tt, LANES), F32),
                        pltpu.VMEM((n_slabs, bsz * tt, LANES), F32),
                        pltpu.VMEM((n_slabs, bsz, LANES), F32)],
        compiler_params=_params(1),
        name="rglru_mixer",
    )(z, w_d, conv_w, conv_b.reshape(1, w), wax, b_a.reshape(1, w), b_x.reshape(1, w),
      lam.reshape(1, w))


MERGE_COLS = 256


def _merge_kernel(final, z_ref, h_ref, ya_ref, yb_ref, yc_ref, yd_ref, wmg_ref, wbr_ref,
                  wout_ref, nw_ref, *rest):
    if final:
        out_ref, merged_ref = rest
    else:
        hn_ref, zn_ref, merged_ref = rest
    zt = z_ref[0]
    ys = [ya_ref[0], yb_ref[0], yc_ref[0], yd_ref[0]]
    for n in range(D_MODEL // MERGE_COLS):
        cols = slice(n * MERGE_COLS, (n + 1) * MERGE_COLS)
        merged = None
        for b in range(N_BRANCH):
            gcols = slice(b * D_MODEL + n * MERGE_COLS, b * D_MODEL + (n + 1) * MERGE_COLS)
            gate = _sigmoid(_dot(zt, wmg_ref[:, gcols]))
            term = gate * _dot(ys[b], wbr_ref[b, :, cols])
            merged = term if merged is None else merged + term
        merged_ref[:, cols] = merged.astype(BF16)
    out = h_ref[0] + _dot(merged_ref[...], wout_ref[...])
    normed = _rms_norm_rows(out, nw_ref[...])
    if final:
        out_ref[0] = normed
    else:
        hn_ref[0] = out
        zn_ref[0] = normed.astype(BF16)


def _merge(z, h, ys, w_mg, w_br, w_out, next_norm_w, final, ts):
    bsz, seqlen, d = h.shape
    f32_out = jax.ShapeDtypeStruct((bsz, seqlen, d), F32)
    if final:
        out_shape, out_specs = f32_out, _row_spec(ts, d)
    else:
        out_shape = (f32_out, jax.ShapeDtypeStruct((bsz, seqlen, d), BF16))
        out_specs = (_row_spec(ts, d), _row_spec(ts, d))
    return pl.pallas_call(
        functools.partial(_merge_kernel, final),
        grid=(bsz, seqlen // ts),
        in_specs=[_row_spec(ts, d), _row_spec(ts, d)] + [_row_spec(ts, BRANCH_W)] * N_BRANCH
                 + [_full_spec(w_mg.shape), _full_spec(w_br.shape), _full_spec(w_out.shape),
                    _full_spec((1, d))],
        out_specs=out_specs,
        out_shape=out_shape,
        scratch_shapes=[pltpu.VMEM((ts, d), BF16)],
        compiler_params=_params(2),
        name="merge_final" if final else "merge",
    )(z, h, *ys, w_mg, w_br, w_out, next_norm_w.reshape(1, d))


TS_NORM = 1024
TS_ATTN = 1024
TS_SG = 512
TT_LRU = 64
TS_MERGE = 256


def kernel(x, norm_w, w_in, hg_lb_logits, hg_norm_w, gla_gk_w, gla_gk_b, gla_norm_w, sg_ln_w, sg_ln_b, sg_w, sg_b, lru_conv_w, lru_conv_b, lru_w_a, lru_b_a, lru_w_x, lru_b_x, lru_lambda, w_branch, w_out, final_norm_w):
    dmat = jnp.asarray(_decay_matrix(), BF16)
    masks = jnp.asarray(_level_masks(), F32)
    lr_pad = jnp.zeros((D_MODEL, LANES - GLA_RANK), BF16)
    gkw_pad = jnp.zeros((LANES - GLA_RANK, GLA_HEADS * GLA_DK), BF16)

    z = _first_norm(x, norm_w[0], TS_NORM)
    h = x
    for l in range(DEPTH):
        w16 = w_in[l].astype(BF16)
        w_a = w16[:, OFF_A:OFF_B]
        w_b = jnp.concatenate([w16[:, OFF_B:OFF_LR], w16[:, OFF_LR:OFF_C], lr_pad], axis=1)
        w_c = w16[:, OFF_C:OFF_D]
        w_d = w16[:, OFF_D:OFF_MG]
        w_mg = w16[:, OFF_MG:N_IN]
        gkw = jnp.concatenate([gla_gk_w[l].astype(BF16), gkw_pad], axis=0)
        wax = jnp.concatenate([lru_w_a[l], lru_w_x[l]], axis=-1).astype(BF16)
        sg_bias = jnp.repeat(sg_b[l].T, LANES, axis=1)

        y_a = _hgrn2(z, w_a, hg_lb_logits, hg_norm_w[l], dmat, masks, l, TS_ATTN)
        y_b = _gla(z, w_b, gkw, gla_gk_b[l], gla_norm_w[l], dmat, masks, TS_ATTN)
        y_c = _spatial_gating(z, w_c, sg_ln_w[l], sg_ln_b[l], sg_w[l], sg_bias, TS_SG)
        y_d = _rglru(z, w_d, lru_conv_w[l], lru_conv_b[l], wax, lru_b_a[l], lru_b_x[l],
                     lru_lambda[l], TT_LRU)
        ys = (y_a, y_b, y_c, y_d)
        w_br = w_branch[l].astype(BF16)
        w_o = w_out[l].astype(BF16)
        if l + 1 < DEPTH:
            h, z = _merge(z, h, ys, w_mg, w_br, w_o, norm_w[l + 1], False, TS_MERGE)
        else:
            return _merge(z, h, ys, w_mg, w_br, w_o, final_norm_w, True, TS_MERGE)
```

```python
import functools

import numpy as np
import jax
import jax.numpy as jnp
from jax import lax
from jax.experimental import pallas as pl
from jax.experimental.pallas import tpu as pltpu

F32 = jnp.float32
BF16 = jnp.bfloat16

D_MODEL = 1024
DEPTH = 2
CHUNK = 64
BRANCH_W = 512
N_BRANCH = 4
NORM_EPS = 1e-6
SQRT_EPS = 1e-12
HG_HEADS = 4
HG_DK = 128
GLA_HEADS = 4
GLA_DK = 64
GLA_RANK = 16
GLA_TAU = 16.0
SG_GROUPS = 4
SG_LEN = 128
LRU_HEADS = 4
LRU_HD = 128
CONV_WIDTH = 4
RG_C = 8.0

LANES = 128
SUBLANES = 8
VMEM_LIMIT_BYTES = 56 * 1024 * 1024
LOG2E = 1.4426950408889634

OFF_A = 0
OFF_B = 2048
OFF_LR = 3584
OFF_C = 3600
OFF_D = 5136
OFF_MG = 6160
N_IN = 10256

LEVELS = (32, 16, 8, 4, 2, 1)
N_DECAY_BLOCKS = 2 + len(LEVELS)


def _decay_matrix():
    c = CHUNK
    i = np.arange(c)[:, None]
    j = np.arange(c)[None, :]
    blocks = [(j <= i), (j > i)]
    for s in LEVELS:
        ref = (i // (2 * s)) * (2 * s) + s - 1
        in_b = i > ref
        blocks.append(np.where(in_b, (j > ref) & (j <= i), (j > i) & (j <= ref)))
    m = np.concatenate(blocks, axis=0).astype(np.float32)
    return np.concatenate([m, m, m], axis=1)


def _level_masks():
    c = CHUNK
    i = np.arange(c)[:, None]
    j = np.arange(c)[None, :]
    masks = []
    for s in LEVELS[:-1]:
        same = (i // (2 * s)) == (j // (2 * s))
        masks.append(same & ((i // s) % 2 == 1) & ((j // s) % 2 == 0))
    return np.stack(masks).astype(np.float32)


def _sigmoid(x):
    return 0.5 + 0.5 * jnp.tanh(0.5 * x)


def _silu(x):
    half = 0.5 * x
    return half + half * jnp.tanh(half)


def _dot(a, b):
    return jnp.dot(a, b, preferred_element_type=F32)


def _dot_nt(a, b):
    return lax.dot_general(a, b, (((1,), (1,)), ((), ())), preferred_element_type=F32)


def _dot_tn(a, b):
    return lax.dot_general(a, b, (((0,), (0,)), ((), ())), preferred_element_type=F32)


def _split3(x):
    hi = x.astype(BF16)
    r1 = x - hi.astype(F32)
    mid = r1.astype(BF16)
    lo = (r1 - mid.astype(F32)).astype(BF16)
    return jnp.concatenate([hi, mid, lo], axis=0)


def _rms_norm_rows(x, w):
    return x * lax.rsqrt(jnp.mean(x * x, axis=-1, keepdims=True) + NORM_EPS) * w


def _params(n_grid_dims):
    return pltpu.CompilerParams(
        dimension_semantics=("arbitrary",) * n_grid_dims,
        vmem_limit_bytes=VMEM_LIMIT_BYTES,
    )


def _full_spec(shape):
    return pl.BlockSpec(shape, lambda *_: (0,) * len(shape))


def _row_spec(ts, width):
    return pl.BlockSpec((1, ts, width), lambda b, s: (b, s, 0))


_W_SEGMENTS = (("hgrn2", 2048, OFF_B - OFF_A), ("gla", 2048, OFF_LR - OFF_B + LANES),
               ("merge", 4096, N_IN - OFF_MG), ("sg", 2048, OFF_D - OFF_C),
               ("lru", 1024, OFF_MG - OFF_D))
_W_SOURCE = {"hgrn2": (OFF_A, OFF_B), "gla": (OFF_B, OFF_C), "merge": (OFF_MG, N_IN),
             "sg": (OFF_C, OFF_D), "lru": (OFF_D, OFF_MG)}


def _pack_in_proj(w_in):
    w16 = w_in.astype(BF16)
    parts = []
    for name, width, _ in _W_SEGMENTS:
        lo, hi = _W_SOURCE[name]
        parts.append(w16[..., lo:hi])
        if width > hi - lo:
            parts.append(jnp.zeros(w16.shape[:2] + (width - (hi - lo),), BF16))
    return jnp.concatenate(parts, axis=-1)


def _w_spec(name, layer, d):
    offset = 0
    for seg, width, _ in _W_SEGMENTS:
        if seg == name:
            assert offset % width == 0
            return pl.BlockSpec((None, d, width), lambda *_: (layer, 0, offset // width))
        offset += width
    raise KeyError(name)


def _w_cols(name):
    return {seg: used for seg, _, used in _W_SEGMENTS}[name]


def _norm_kernel(x_ref, w_ref, z_ref):
    z_ref[0] = _rms_norm_rows(x_ref[0], w_ref[...]).astype(BF16)


def _first_norm(x, w, ts):
    bsz, seqlen, d = x.shape
    return pl.pallas_call(
        _norm_kernel,
        grid=(bsz, seqlen // ts),
        in_specs=[_row_spec(ts, d), _full_spec((1, d))],
        out_specs=_row_spec(ts, d),
        out_shape=jax.ShapeDtypeStruct((bsz, seqlen, d), BF16),
        compiler_params=_params(2),
        name="first_norm",
    )(x, w.reshape(1, d))


def _interleave(*pieces):
    live = list(pieces)
    while live:
        for p in list(live):
            try:
                next(p)
            except StopIteration:
                live.remove(p)


def _pipelined_chunks(n_chunks, decay_pieces, score_pieces, tail_pieces):
    assert n_chunks % 2 == 0 and n_chunks >= 4
    _interleave(decay_pieces(0, 0))
    _interleave(decay_pieces(1, 1), score_pieces(0, 0))

    def body(i, carry):
        h = 2 * i + 1
        _interleave(decay_pieces(h + 1, 0), score_pieces(h, 1), tail_pieces(h - 1, 0))
        _interleave(decay_pieces(h + 2, 1), score_pieces(h + 1, 0), tail_pieces(h, 1))
        return carry

    lax.fori_loop(0, (n_chunks - 2) // 2, body, 0)
    _interleave(score_pieces(n_chunks - 1, 1), tail_pieces(n_chunks - 2, 0))
    _interleave(tail_pieces(n_chunks - 1, 1))


def _chunk_rows(ci):
    start = ci * CHUNK
    if not isinstance(ci, int):
        start = pl.multiple_of(start, CHUNK)
    return pl.ds(start, CHUNK)


def _decay_pieces(log_decay, dmat_ref, d_ref, e_ref, elast_ref):
    d_ref[...] = _dot(dmat_ref[...], _split3(log_decay * LOG2E))
    yield
    for blk in range(N_DECAY_BLOCKS):
        rows = slice(blk * CHUNK, (blk + 1) * CHUNK)
        ex = jnp.exp2(d_ref[rows, :])
        e_ref[rows, :] = ex.astype(BF16)
        if blk == 0:
            elast_ref[...] = ex[CHUNK - SUBLANES:CHUNK, :]
        yield


def _score_pieces(q_units, k_units, v_heads, heads_per_unit, masks_ref, e_ref, elast_ref,
                  state_ref, o_ref):
    c = CHUNK
    n_heads = len(v_heads)
    lane = lax.broadcasted_iota(jnp.int32, (c, LANES), 1)
    head_w = LANES // heads_per_unit
    unit_lanes = [slice(u * LANES, (u + 1) * LANES) for u in range(len(q_units))]
    k16_units = [k.astype(BF16) for k in k_units]
    qf_heads, q16_heads, kown_heads = [], [], []
    for h in range(n_heads):
        u = h // heads_per_unit
        qf = q_units[u]
        k16 = k16_units[u]
        if heads_per_unit > 1:
            sub = h % heads_per_unit
            in_head = (lane >= sub * head_w) & (lane < (sub + 1) * head_w)
            qf = jnp.where(in_head, qf, 0.0)
            kown_heads.append(jnp.where(in_head, k16, jnp.zeros_like(k16)))
        else:
            kown_heads.append(k16)
        qf_heads.append(qf)
        q16_heads.append(qf.astype(BF16))
    v16_heads = [v.astype(BF16) for v in v_heads]
    yield

    pair_blk = 1 + len(LEVELS)
    odd_row = jnp.bitwise_and(lax.broadcasted_iota(jnp.int32, (c, 1), 0), 1) == 1
    e_pair_units = [e_ref[pair_blk * c:(pair_blk + 1) * c, ul].astype(F32) for ul in unit_lanes]
    k_prev_units = [pltpu.roll(k, 1, axis=0) for k in k_units]
    fine = []
    for h in range(n_heads):
        u = h // heads_per_unit
        s_diag = jnp.sum(qf_heads[h] * k_units[u], axis=-1, keepdims=True)
        s_pair = jnp.sum(qf_heads[h] * e_pair_units[u] * k_prev_units[u], axis=-1, keepdims=True)
        s_pair = jnp.where(odd_row, s_pair, 0.0)
        fine.append(s_diag * v_heads[h] + s_pair * pltpu.roll(v_heads[h], 1, axis=0))
    yield

    scores = [None] * n_heads
    for lvl in range(len(LEVELS) - 1):
        e_units = [e_ref[(2 + lvl) * c:(3 + lvl) * c, ul] for ul in unit_lanes]
        ke_units = [k16_units[u] * e_units[u] for u in range(len(unit_lanes))]
        for h in range(n_heads):
            u = h // heads_per_unit
            term = masks_ref[lvl] * _dot_nt(q16_heads[h] * e_units[u], ke_units[u])
            scores[h] = term if scores[h] is None else scores[h] + term
        yield

    states = [state_ref[h] for h in range(n_heads)]
    for h in range(n_heads):
        e_in = e_ref[0:c, unit_lanes[h // heads_per_unit]]
        o_ref[:, h * LANES:(h + 1) * LANES] = (
            _dot(scores[h].astype(BF16), v16_heads[h])
            + _dot_nt(q16_heads[h] * e_in, states[h].astype(BF16)) + fine[h])
    yield
    for h in range(n_heads):
        lanes = unit_lanes[h // heads_per_unit]
        e_out = e_ref[c:2 * c, lanes]
        e_last = elast_ref[SUBLANES - 1:SUBLANES, lanes]
        state_ref[h] = states[h] * e_last + _dot_tn(v16_heads[h], kown_heads[h] * e_out)
    yield


def _head_norm_gate(o, norm_w, gate):
    y = o * lax.rsqrt(jnp.mean(o * o, axis=-1, keepdims=True) + NORM_EPS) * norm_w
    return y * _silu(gate)


def _attention_scratch(ts, n_cols, decay_w, n_heads):
    return [pltpu.VMEM((ts, n_cols), F32),
            pltpu.VMEM((N_DECAY_BLOCKS * CHUNK, decay_w), F32),
            pltpu.VMEM((2, N_DECAY_BLOCKS * CHUNK, decay_w), BF16),
            pltpu.VMEM((2, SUBLANES, decay_w), F32),
            pltpu.VMEM((2, CHUNK, n_heads * LANES), F32),
            pltpu.VMEM((n_heads, LANES, LANES), F32)]


def _hgrn2_kernel(layer, z_ref, w_ref, lbl_ref, nw_ref, dmat_ref, masks_ref, y_ref,
                  proj_ref, d_ref, e_ref, elast_ref, o_ref, state_ref, key_ref):
    @pl.when(pl.program_id(1) == 0)
    def _():
        state_ref[...] = jnp.zeros_like(state_ref)

    proj_ref[...] = _dot(z_ref[0], w_ref[...])

    logits = [lbl_ref[i:i + 1, :] for i in range(DEPTH)]
    mx = functools.reduce(jnp.maximum, logits)
    ex = [jnp.exp(t - mx) for t in logits]
    tot = functools.reduce(lambda a, b: a + b, ex)
    lb = functools.reduce(lambda a, b: a + b, ex[:layer + 1]) / tot - ex[0] / tot
    nw = nw_ref[...]
    w = BRANCH_W
    ts = proj_ref.shape[0]
    units = [slice(h * LANES, (h + 1) * LANES) for h in range(HG_HEADS)]

    def decay_pieces(ci, slot):
        rows = _chunk_rows(ci)
        zf = proj_ref[rows, w:2 * w]
        en = jnp.exp(-jnp.abs(zf))
        r = 1.0 / (1.0 + en)
        er = en * r
        pos = zf >= 0.0
        sig = jnp.where(pos, r, er)
        nsig = jnp.where(pos, er, r)
        key_ref[slot] = (1.0 - lb) * nsig
        log_f = jnp.log(lb + (1.0 - lb) * sig)
        yield
        yield from _decay_pieces(log_f, dmat_ref, d_ref, e_ref.at[slot], elast_ref.at[slot])

    def score_pieces(ci, slot):
        rows = _chunk_rows(ci)
        q = proj_ref[rows, 0:w] * (HG_DK ** -0.5)
        key = key_ref[slot]
        vin = proj_ref[rows, 2 * w:3 * w]
        yield from _score_pieces([q[:, u] for u in units], [key[:, u] for u in units],
                                 [vin[:, u] for u in units], 1, masks_ref, e_ref.at[slot],
                                 elast_ref.at[slot], state_ref, o_ref.at[slot])

    def tail_pieces(ci, slot):
        rows = _chunk_rows(ci)
        for h, u in enumerate(units):
            gate = proj_ref[rows, 3 * w + h * LANES:3 * w + (h + 1) * LANES]
            y_ref[0, rows, u] = _head_norm_gate(o_ref[slot, :, u], nw, gate).astype(BF16)
            yield

    _pipelined_chunks(ts // CHUNK, decay_pieces, score_pieces, tail_pieces)


def _hgrn2(z, w_all, lb_logits, norm_w, dmat, masks, layer, ts):
    bsz, seqlen, d = z.shape
    n_cols = _w_cols("hgrn2")
    return pl.pallas_call(
        functools.partial(_hgrn2_kernel, layer),
        grid=(bsz, seqlen // ts),
        in_specs=[_row_spec(ts, d), _w_spec("hgrn2", layer, d), _full_spec(lb_logits.shape),
                  _full_spec((1, LANES)), _full_spec(dmat.shape), _full_spec(masks.shape)],
        out_specs=_row_spec(ts, BRANCH_W),
        out_shape=jax.ShapeDtypeStruct((bsz, seqlen, BRANCH_W), BF16),
        scratch_shapes=_attention_scratch(ts, n_cols, BRANCH_W, HG_HEADS)
                       + [pltpu.VMEM((2, CHUNK, BRANCH_W), F32)],
        compiler_params=_params(2),
        name="hgrn2_mixer",
    )(z, w_all, lb_logits, norm_w.reshape(1, LANES), dmat, masks)


def _gla_kernel(z_ref, w_ref, gkw_ref, gkb_ref, nw_ref, dmat_ref, masks_ref, y_ref,
                proj_ref, d_ref, e_ref, elast_ref, o_ref, state_ref):
    @pl.when(pl.program_id(1) == 0)
    def _():
        state_ref[...] = jnp.zeros_like(state_ref)

    proj_ref[...] = _dot(z_ref[0], w_ref[:, 0:proj_ref.shape[1]])
    nw = nw_ref[...]
    gkb = gkb_ref[...]
    qk_w = GLA_HEADS * GLA_DK
    w = BRANCH_W
    ts = proj_ref.shape[0]
    units = [slice(u * LANES, (u + 1) * LANES) for u in range(qk_w // LANES)]
    heads = [slice(h * LANES, (h + 1) * LANES) for h in range(GLA_HEADS)]
    gate_off = 2 * qk_w + w

    def decay_pieces(ci, slot):
        rows = _chunk_rows(ci)
        lowrank = proj_ref[rows, gate_off + w:gate_off + w + LANES]
        gk = _dot(lowrank.astype(BF16), gkw_ref[...]) + gkb
        log_alpha = (jnp.minimum(gk, 0.0) - jnp.log(1.0 + jnp.exp(-jnp.abs(gk)))) * (1.0 / GLA_TAU)
        yield
        yield from _decay_pieces(log_alpha, dmat_ref, d_ref, e_ref.at[slot], elast_ref.at[slot])

    def score_pieces(ci, slot):
        rows = _chunk_rows(ci)
        q = proj_ref[rows, 0:qk_w] * (GLA_DK ** -0.5)
        k = proj_ref[rows, qk_w:2 * qk_w]
        v = proj_ref[rows, 2 * qk_w:2 * qk_w + w]
        yield from _score_pieces([q[:, u] for u in units], [k[:, u] for u in units],
                                 [v[:, hs] for hs in heads], LANES // GLA_DK, masks_ref,
                                 e_ref.at[slot], elast_ref.at[slot], state_ref, o_ref.at[slot])

    def tail_pieces(ci, slot):
        rows = _chunk_rows(ci)
        for h, hs in enumerate(heads):
            gate = proj_ref[rows, gate_off + h * LANES:gate_off + (h + 1) * LANES]
            y_ref[0, rows, hs] = _head_norm_gate(o_ref[slot, :, hs], nw, gate).astype(BF16)
            yield

    _pipelined_chunks(ts // CHUNK, decay_pieces, score_pieces, tail_pieces)


def _gla(z, w_all, gk_w, gk_b, norm_w, dmat, masks, layer, ts):
    bsz, seqlen, d = z.shape
    n_cols = _w_cols("gla")
    qk_w = GLA_HEADS * GLA_DK
    return pl.pallas_call(
        _gla_kernel,
        grid=(bsz, seqlen // ts),
        in_specs=[_row_spec(ts, d), _w_spec("gla", layer, d), _full_spec((LANES, qk_w)),
                  _full_spec((1, qk_w)), _full_spec((1, LANES)), _full_spec(dmat.shape),
                  _full_spec(masks.shape)],
        out_specs=_row_spec(ts, BRANCH_W),
        out_shape=jax.ShapeDtypeStruct((bsz, seqlen, BRANCH_W), BF16),
        scratch_shapes=_attention_scratch(ts, n_cols, qk_w, GLA_HEADS),
        compiler_params=_params(2),
        name="gla_mixer",
    )(z, w_all, gk_w, gk_b.reshape(1, qk_w), norm_w.reshape(1, LANES), dmat, masks)


def _gelu(x):
    return 0.5 * x * (1.0 + lax.erf(x * (2.0 ** -0.5)))


def _sg_kernel(z_ref, w_ref, lnw_ref, lnb_ref, ws_ref, bias_ref, y_ref):
    w = BRANCH_W
    proj = _dot(z_ref[0], w_ref[:, 0:3 * w])
    ts = proj.shape[0]
    u = _gelu(proj[:, 0:w])
    vf = _gelu(proj[:, w:2 * w])
    gate = proj[:, 2 * w:3 * w]
    mu = jnp.mean(vf, axis=-1, keepdims=True)
    dv = vf - mu
    var = jnp.mean(dv * dv, axis=-1, keepdims=True)
    vn = (dv * lax.rsqrt(var + NORM_EPS) * lnw_ref[...] + lnb_ref[...]).astype(BF16)
    ri = lax.broadcasted_iota(jnp.int32, (SG_LEN, SG_LEN), 0)
    cj = lax.broadcasted_iota(jnp.int32, (SG_LEN, SG_LEN), 1)
    allowed = (cj < CHUNK) | (ri >= CHUNK)
    wm = [jnp.where(allowed, ws_ref[g], 0.0).astype(BF16) for g in range(SG_GROUPS)]
    bias = bias_ref[...]
    ug = u * _silu(gate)
    for blk in range(ts // SG_LEN):
        rows = slice(blk * SG_LEN, (blk + 1) * SG_LEN)
        for g in range(SG_GROUPS):
            cols = slice(g * LANES, (g + 1) * LANES)
            mixed = _dot(wm[g], vn[rows, cols]) + bias[:, cols]
            y_ref[0, rows, cols] = (ug[rows, cols] * mixed).astype(BF16)


def _spatial_gating(z, w_all, ln_w, ln_b, w_s, bias, layer, ts):
    bsz, seqlen, d = z.shape
    assert _w_cols("sg") == 3 * BRANCH_W
    return pl.pallas_call(
        _sg_kernel,
        grid=(bsz, seqlen // ts),
        in_specs=[_row_spec(ts, d), _w_spec("sg", layer, d), _full_spec((1, BRANCH_W)),
                  _full_spec((1, BRANCH_W)), _full_spec(w_s.shape), _full_spec(bias.shape)],
        out_specs=_row_spec(ts, BRANCH_W),
        out_shape=jax.ShapeDtypeStruct((bsz, seqlen, BRANCH_W), BF16),
        compiler_params=_params(2),
        name="spatial_gating_mixer",
    )(z, w_all, ln_w.reshape(1, BRANCH_W), ln_b.reshape(1, BRANCH_W), w_s, bias)


def _expm1_nonpos(x):
    p = 1.0 / 720
    for coeff in (1.0 / 120, 1.0 / 24, 1.0 / 6, 0.5, 1.0):
        p = p * x + coeff
    return jnp.where(x > -0.125, p * x, jnp.exp(x) - 1.0)


def _lru_kernel(z_ref, w_ref, cw_ref, cb_ref, wax_ref, ba_ref, bx_ref, lam_ref, y_ref,
                xext_ref, ab_ref, hs_ref, hprev_ref):
    bsz, tt, d = z_ref.shape
    w = BRANCH_W
    pad = SUBLANES
    n_rows = bsz * tt
    n_slabs = w // LANES

    @pl.when(pl.program_id(0) == 0)
    def _():
        xext_ref[:, 0:pad, :] = jnp.zeros((bsz, pad, w), F32)
        hprev_ref[...] = jnp.zeros_like(hprev_ref)

    proj = _dot(z_ref[...].reshape(n_rows, d), w_ref[...])
    xr = proj[:, 0:w]
    gate = proj[:, w:2 * w]
    xr3 = xr.reshape(bsz, tt, w)
    xext_ref[:, pad:pad + tt, :] = xr3
    xc = cb_ref[...]
    for tap in range(CONV_WIDTH):
        back = CONV_WIDTH - 1 - tap
        xs = xr if back == 0 else xext_ref[:, pad - back:pad - back + tt, :].reshape(n_rows, w)
        xc = xc + xs * cw_ref[tap:tap + 1, :]
    xext_ref[:, 0:pad, :] = xr3[:, tt - pad:tt, :]

    xc16 = xc.astype(BF16)
    ra, ix = [], []
    for h in range(LRU_HEADS):
        cols = slice(h * LRU_HD, (h + 1) * LRU_HD)
        both = _dot(xc16[:, cols], wax_ref[h])
        ra.append(both[:, 0:LRU_HD])
        ix.append(both[:, LRU_HD:2 * LRU_HD])
    ig = _sigmoid(jnp.concatenate(ix, axis=1) + bx_ref[...])
    nlam = -lam_ref[...]
    softplus = jnp.maximum(nlam, 0.0) + jnp.log(1.0 + jnp.exp(-jnp.abs(nlam)))
    half_scale = (-0.5 * RG_C) * softplus
    log_a = half_scale + half_scale * jnp.tanh(0.5 * (jnp.concatenate(ra, axis=1) + ba_ref[...]))
    a = jnp.exp(log_a)
    var = jnp.maximum(-_expm1_nonpos(2.0 * log_a), SQRT_EPS)
    bt = (var * lax.rsqrt(var)) * (ig * xc)

    for b in range(bsz):
        rows = slice(b * tt, (b + 1) * tt)
        dst = pl.ds(b, tt, stride=bsz)
        for j in range(n_slabs):
            cols = slice(j * LANES, (j + 1) * LANES)
            ab_ref[0, j, dst, :] = a[rows, cols]
            ab_ref[1, j, dst, :] = bt[rows, cols]

    hcur = [hprev_ref[j] for j in range(n_slabs)]
    for t in range(tt):
        step = slice(t * bsz, (t + 1) * bsz)
        for j in range(n_slabs):
            hcur[j] = ab_ref[0, j, step, :] * hcur[j] + ab_ref[1, j, step, :]
            hs_ref[j, step, :] = hcur[j]
    for j in range(n_slabs):
        hprev_ref[j] = hcur[j]

    hseq = jnp.concatenate(
        [jnp.concatenate([hs_ref[j, pl.ds(b, tt, stride=bsz), :] for j in range(n_slabs)], axis=1)
         for b in range(bsz)], axis=0)
    y_ref[...] = (hseq * _silu(gate)).astype(BF16).reshape(bsz, tt, w)


def _rglru(z, w_all, conv_w, conv_b, wax, b_a, b_x, lam, layer, tt):
    bsz, seqlen, d = z.shape
    w = BRANCH_W
    assert _w_cols("lru") == 2 * w
    assert bsz == SUBLANES, "the RG-LRU scan puts the batch on the vreg sublanes"
    n_slabs = w // LANES
    return pl.pallas_call(
        _lru_kernel,
        grid=(seqlen // tt,),
        in_specs=[pl.BlockSpec((bsz, tt, d), lambda i: (0, i, 0)), _w_spec("lru", layer, d),
                  _full_spec((CONV_WIDTH, w)), _full_spec((1, w)), _full_spec(wax.shape),
                  _full_spec((1, w)), _full_spec((1, w)), _full_spec((1, w))],
        out_specs=pl.BlockSpec((bsz, tt, w), lambda i: (0, i, 0)),
        out_shape=jax.ShapeDtypeStruct((bsz, seqlen, w), BF16),
        scratch_shapes=[pltpu.VMEM((bsz, tt + SUBLANES, w), F32),
                        pltpu.VMEM((2, n_slabs, bsz * tt, LANES), F32),
                        pltpu.VMEM((n_slabs, bsz * tt, LANES), F32),
                        pltpu.VMEM((n_slabs, bsz, LANES), F32)],
        compiler_params=_params(1),
        name="rglru_mixer",
    )(z, w_all, conv_w, conv_b.reshape(1, w), wax, b_a.reshape(1, w), b_x.reshape(1, w),
      lam.reshape(1, w))


MERGE_COLS = 256


def _merge_kernel(final, z_ref, h_ref, ya_ref, yb_ref, yc_ref, yd_ref, wmg_ref, wbr_ref,
                  wout_ref, nw_ref, *rest):
    if final:
        out_ref, merged_ref = rest
    else:
        hn_ref, zn_ref, merged_ref = rest
    zt = z_ref[0]
    ys = [ya_ref[0], yb_ref[0], yc_ref[0], yd_ref[0]]
    for n in range(D_MODEL // MERGE_COLS):
        cols = slice(n * MERGE_COLS, (n + 1) * MERGE_COLS)
        merged = None
        for b in range(N_BRANCH):
            gcols = slice(b * D_MODEL + n * MERGE_COLS, b * D_MODEL + (n + 1) * MERGE_COLS)
            gate = _sigmoid(_dot(zt, wmg_ref[:, gcols]))
            term = gate * _dot(ys[b], wbr_ref[b, :, cols])
            merged = term if merged is None else merged + term
        merged_ref[:, cols] = merged.astype(BF16)
    out = h_ref[0] + _dot(merged_ref[...], wout_ref[...])
    normed = _rms_norm_rows(out, nw_ref[...])
    if final:
        out_ref[0] = normed
    else:
        hn_ref[0] = out
        zn_ref[0] = normed.astype(BF16)


def _merge(z, h, ys, w_all, w_br, w_out, next_norm_w, layer, final, ts):
    bsz, seqlen, d = h.shape
    assert _w_cols("merge") == N_BRANCH * d
    f32_out = jax.ShapeDtypeStruct((bsz, seqlen, d), F32)
    if final:
        out_shape, out_specs = f32_out, _row_spec(ts, d)
    else:
        out_shape = (f32_out, jax.ShapeDtypeStruct((bsz, seqlen, d), BF16))
        out_specs = (_row_spec(ts, d), _row_spec(ts, d))
    return pl.pallas_call(
        functools.partial(_merge_kernel, final),
        grid=(bsz, seqlen // ts),
        in_specs=[_row_spec(ts, d), _row_spec(ts, d)] + [_row_spec(ts, BRANCH_W)] * N_BRANCH
                 + [_w_spec("merge", layer, d),
                    pl.BlockSpec((None,) + w_br.shape[1:], lambda *_: (layer, 0, 0, 0)),
                    pl.BlockSpec((None,) + w_out.shape[1:], lambda *_: (layer, 0, 0)),
                    _full_spec((1, d))],
        out_specs=out_specs,
        out_shape=out_shape,
        scratch_shapes=[pltpu.VMEM((ts, d), BF16)],
        compiler_params=_params(2),
        name="merge_final" if final else "merge",
    )(z, h, *ys, w_all, w_br, w_out, next_norm_w.reshape(1, d))


TS_NORM = 1024
TS_ATTN = 1024
TS_SG = 512
TT_LRU = 64
TS_MERGE = 256


def kernel(x, norm_w, w_in, hg_lb_logits, hg_norm_w, gla_gk_w, gla_gk_b, gla_norm_w, sg_ln_w, sg_ln_b, sg_w, sg_b, lru_conv_w, lru_conv_b, lru_w_a, lru_b_a, lru_w_x, lru_b_x, lru_lambda, w_branch, w_out, final_norm_w):
    dmat = jnp.asarray(_decay_matrix(), BF16)
    masks = jnp.asarray(_level_masks(), F32)
    gkw_pad = jnp.zeros((LANES - GLA_RANK, GLA_HEADS * GLA_DK), BF16)
    w_all = _pack_in_proj(w_in)
    w_br = w_branch.astype(BF16)
    w_o = w_out.astype(BF16)

    z = _first_norm(x, norm_w[0], TS_NORM)
    h = x
    for l in range(DEPTH):
        gkw = jnp.concatenate([gla_gk_w[l].astype(BF16), gkw_pad], axis=0)
        wax = jnp.concatenate([lru_w_a[l], lru_w_x[l]], axis=-1).astype(BF16)
        sg_bias = jnp.repeat(sg_b[l].T, LANES, axis=1)

        y_a = _hgrn2(z, w_all, hg_lb_logits, hg_norm_w[l], dmat, masks, l, TS_ATTN)
        y_b = _gla(z, w_all, gkw, gla_gk_b[l], gla_norm_w[l], dmat, masks, l, TS_ATTN)
        y_c = _spatial_gating(z, w_all, sg_ln_w[l], sg_ln_b[l], sg_w[l], sg_bias, l, TS_SG)
        y_d = _rglru(z, w_all, lru_conv_w[l], lru_conv_b[l], wax, lru_b_a[l], lru_b_x[l],
                     lru_lambda[l], l, TT_LRU)
        ys = (y_a, y_b, y_c, y_d)
        if l + 1 < DEPTH:
            h, z = _merge(z, h, ys, w_all, w_br, w_o, norm_w[l + 1], l, False, TS_MERGE)
        else:
            return _merge(z, h, ys, w_all, w_br, w_o, final_norm_w, l, True, TS_MERGE)
```

```python
import functools

import numpy as np
import jax
import jax.numpy as jnp
from jax import lax
from jax.experimental import pallas as pl
from jax.experimental.pallas import tpu as pltpu

F32 = jnp.float32
BF16 = jnp.bfloat16

D_MODEL = 1024
DEPTH = 2
CHUNK = 64
BRANCH_W = 512
N_BRANCH = 4
NORM_EPS = 1e-6
SQRT_EPS = 1e-12
HG_HEADS = 4
HG_DK = 128
GLA_HEADS = 4
GLA_DK = 64
GLA_RANK = 16
GLA_TAU = 16.0
SG_GROUPS = 4
SG_LEN = 128
LRU_HEADS = 4
LRU_HD = 128
CONV_WIDTH = 4
RG_C = 8.0

LANES = 128
SUBLANES = 8
VMEM_LIMIT_BYTES = 56 * 1024 * 1024
LOG2E = 1.4426950408889634

OFF_A = 0
OFF_B = 2048
OFF_LR = 3584
OFF_C = 3600
OFF_D = 5136
OFF_MG = 6160
N_IN = 10256

LEVELS = (32, 16, 8, 4, 2, 1)
N_DECAY_BLOCKS = 2 + len(LEVELS)


def _decay_matrix():
    c = CHUNK
    i = np.arange(c)[:, None]
    j = np.arange(c)[None, :]
    blocks = [(j <= i), (j > i)]
    for s in LEVELS:
        ref = (i // (2 * s)) * (2 * s) + s - 1
        in_b = i > ref
        blocks.append(np.where(in_b, (j > ref) & (j <= i), (j > i) & (j <= ref)))
    m = np.concatenate(blocks, axis=0).astype(np.float32)
    return np.concatenate([m, m, m], axis=1)


def _level_masks():
    c = CHUNK
    i = np.arange(c)[:, None]
    j = np.arange(c)[None, :]
    masks = []
    for s in LEVELS[:-1]:
        same = (i // (2 * s)) == (j // (2 * s))
        masks.append(same & ((i // s) % 2 == 1) & ((j // s) % 2 == 0))
    return np.stack(masks).astype(np.float32)


def _sigmoid(x):
    return 0.5 + 0.5 * jnp.tanh(0.5 * x)


def _silu(x):
    half = 0.5 * x
    return half + half * jnp.tanh(half)


def _dot(a, b):
    return jnp.dot(a, b, preferred_element_type=F32)


def _dot_nt(a, b):
    return lax.dot_general(a, b, (((1,), (1,)), ((), ())), preferred_element_type=F32)


def _dot_tn(a, b):
    return lax.dot_general(a, b, (((0,), (0,)), ((), ())), preferred_element_type=F32)


def _split3(x):
    hi = x.astype(BF16)
    r1 = x - hi.astype(F32)
    mid = r1.astype(BF16)
    lo = (r1 - mid.astype(F32)).astype(BF16)
    return jnp.concatenate([hi, mid, lo], axis=0)


def _rms_norm_rows(x, w):
    return x * lax.rsqrt(jnp.mean(x * x, axis=-1, keepdims=True) + NORM_EPS) * w


def _params(n_grid_dims):
    return pltpu.CompilerParams(
        dimension_semantics=("arbitrary",) * n_grid_dims,
        vmem_limit_bytes=VMEM_LIMIT_BYTES,
    )


def _full_spec(shape):
    return pl.BlockSpec(shape, lambda *_: (0,) * len(shape))


def _row_spec(ts, width):
    return pl.BlockSpec((1, ts, width), lambda b, s: (b, s, 0))


_W_SEGMENTS = (("hgrn2", 2048, OFF_B - OFF_A), ("gla", 2048, OFF_LR - OFF_B + LANES),
               ("merge", 4096, N_IN - OFF_MG), ("sg", 2048, OFF_D - OFF_C),
               ("lru", 1024, OFF_MG - OFF_D))
_W_SOURCE = {"hgrn2": (OFF_A, OFF_B), "gla": (OFF_B, OFF_C), "merge": (OFF_MG, N_IN),
             "sg": (OFF_C, OFF_D), "lru": (OFF_D, OFF_MG)}


def _pack_kernel(w_ref, o_ref):
    offset = 0
    for name, width, _ in _W_SEGMENTS:
        lo, hi = _W_SOURCE[name]
        o_ref[:, offset:offset + hi - lo] = w_ref[:, lo:hi].astype(BF16)
        if width > hi - lo:
            o_ref[:, offset + hi - lo:offset + width] = jnp.zeros(
                (o_ref.shape[0], width - (hi - lo)), BF16)
        offset += width


def _pack_in_proj(w_in, rows):
    depth, d, n_in = w_in.shape
    packed = sum(width for _, width, _ in _W_SEGMENTS)
    return pl.pallas_call(
        _pack_kernel,
        grid=(depth, d // rows),
        in_specs=[pl.BlockSpec((None, rows, n_in), lambda l, r: (l, r, 0))],
        out_specs=pl.BlockSpec((None, rows, packed), lambda l, r: (l, r, 0)),
        out_shape=jax.ShapeDtypeStruct((depth, d, packed), BF16),
        compiler_params=_params(2),
        name="pack_in_proj",
    )(w_in)


def _w_spec(name, layer, d):
    offset = 0
    for seg, width, _ in _W_SEGMENTS:
        if seg == name:
            assert offset % width == 0
            return pl.BlockSpec((None, d, width), lambda *_: (layer, 0, offset // width),
                                pipeline_mode=pl.Buffered(1))
        offset += width
    raise KeyError(name)


def _w_cols(name):
    return {seg: used for seg, _, used in _W_SEGMENTS}[name]


def _norm_kernel(x_ref, w_ref, z_ref):
    z_ref[0] = _rms_norm_rows(x_ref[0], w_ref[...]).astype(BF16)


def _first_norm(x, w, ts):
    bsz, seqlen, d = x.shape
    return pl.pallas_call(
        _norm_kernel,
        grid=(bsz, seqlen // ts),
        in_specs=[_row_spec(ts, d), _full_spec((1, d))],
        out_specs=_row_spec(ts, d),
        out_shape=jax.ShapeDtypeStruct((bsz, seqlen, d), BF16),
        compiler_params=_params(2),
        name="first_norm",
    )(x, w.reshape(1, d))


def _interleave(*pieces):
    live = list(pieces)
    while live:
        for p in list(live):
            try:
                next(p)
            except StopIteration:
                live.remove(p)


def _pipelined_chunks(n_chunks, decay_pieces, score_pieces, tail_pieces):
    assert n_chunks % 2 == 0 and n_chunks >= 4
    _interleave(decay_pieces(0, 0))
    _interleave(decay_pieces(1, 1), score_pieces(0, 0))

    def body(i, carry):
        h = 2 * i + 1
        _interleave(decay_pieces(h + 1, 0), score_pieces(h, 1), tail_pieces(h - 1, 0))
        _interleave(decay_pieces(h + 2, 1), score_pieces(h + 1, 0), tail_pieces(h, 1))
        return carry

    lax.fori_loop(0, (n_chunks - 2) // 2, body, 0)
    _interleave(score_pieces(n_chunks - 1, 1), tail_pieces(n_chunks - 2, 0))
    _interleave(tail_pieces(n_chunks - 1, 1))


def _chunk_rows(ci):
    start = ci * CHUNK
    if not isinstance(ci, int):
        start = pl.multiple_of(start, CHUNK)
    return pl.ds(start, CHUNK)


def _decay_pieces(log_decay, dmat_ref, d_ref, e_ref, elast_ref):
    d_ref[...] = _dot(dmat_ref[...], _split3(log_decay * LOG2E))
    yield
    for blk in range(N_DECAY_BLOCKS):
        rows = slice(blk * CHUNK, (blk + 1) * CHUNK)
        ex = jnp.exp2(d_ref[rows, :])
        e_ref[rows, :] = ex.astype(BF16)
        if blk == 0:
            elast_ref[...] = ex[CHUNK - SUBLANES:CHUNK, :]
        yield


def _score_pieces(q_units, k_units, v_heads, heads_per_unit, masks_ref, e_ref, elast_ref,
                  state_ref, o_ref):
    c = CHUNK
    n_heads = len(v_heads)
    lane = lax.broadcasted_iota(jnp.int32, (c, LANES), 1)
    head_w = LANES // heads_per_unit
    unit_lanes = [slice(u * LANES, (u + 1) * LANES) for u in range(len(q_units))]
    k16_units = [k.astype(BF16) for k in k_units]
    qf_heads, q16_heads, kown_heads = [], [], []
    for h in range(n_heads):
        u = h // heads_per_unit
        qf = q_units[u]
        k16 = k16_units[u]
        if heads_per_unit > 1:
            sub = h % heads_per_unit
            in_head = (lane >= sub * head_w) & (lane < (sub + 1) * head_w)
            qf = jnp.where(in_head, qf, 0.0)
            kown_heads.append(jnp.where(in_head, k16, jnp.zeros_like(k16)))
        else:
            kown_heads.append(k16)
        qf_heads.append(qf)
        q16_heads.append(qf.astype(BF16))
    v16_heads = [v.astype(BF16) for v in v_heads]
    yield

    pair_blk = 1 + len(LEVELS)
    odd_row = jnp.bitwise_and(lax.broadcasted_iota(jnp.int32, (c, 1), 0), 1) == 1
    e_pair_units = [e_ref[pair_blk * c:(pair_blk + 1) * c, ul].astype(F32) for ul in unit_lanes]
    k_prev_units = [pltpu.roll(k, 1, axis=0) for k in k_units]
    fine = []
    for h in range(n_heads):
        u = h // heads_per_unit
        s_diag = jnp.sum(qf_heads[h] * k_units[u], axis=-1, keepdims=True)
        s_pair = jnp.sum(qf_heads[h] * e_pair_units[u] * k_prev_units[u], axis=-1, keepdims=True)
        s_pair = jnp.where(odd_row, s_pair, 0.0)
        fine.append(s_diag * v_heads[h] + s_pair * pltpu.roll(v_heads[h], 1, axis=0))
    yield

    scores = [None] * n_heads
    for lvl in range(len(LEVELS) - 1):
        e_units = [e_ref[(2 + lvl) * c:(3 + lvl) * c, ul] for ul in unit_lanes]
        ke_units = [k16_units[u] * e_units[u] for u in range(len(unit_lanes))]
        for h in range(n_heads):
            u = h // heads_per_unit
            term = masks_ref[lvl] * _dot_nt(q16_heads[h] * e_units[u], ke_units[u])
            scores[h] = term if scores[h] is None else scores[h] + term
        yield

    states = [state_ref[h] for h in range(n_heads)]
    for h in range(n_heads):
        e_in = e_ref[0:c, unit_lanes[h // heads_per_unit]]
        o_ref[:, h * LANES:(h + 1) * LANES] = (
            _dot(scores[h].astype(BF16), v16_heads[h])
            + _dot_nt(q16_heads[h] * e_in, states[h].astype(BF16)) + fine[h])
    yield
    for h in range(n_heads):
        lanes = unit_lanes[h // heads_per_unit]
        e_out = e_ref[c:2 * c, lanes]
        e_last = elast_ref[SUBLANES - 1:SUBLANES, lanes]
        state_ref[h] = states[h] * e_last + _dot_tn(v16_heads[h], kown_heads[h] * e_out)
    yield


def _head_norm_gate(o, norm_w, gate):
    y = o * lax.rsqrt(jnp.mean(o * o, axis=-1, keepdims=True) + NORM_EPS) * norm_w
    return y * _silu(gate)


def _attention_scratch(ts, n_cols, decay_w, n_heads):
    return [pltpu.VMEM((ts, n_cols), F32),
            pltpu.VMEM((N_DECAY_BLOCKS * CHUNK, decay_w), F32),
            pltpu.VMEM((2, N_DECAY_BLOCKS * CHUNK, decay_w), BF16),
            pltpu.VMEM((2, SUBLANES, decay_w), F32),
            pltpu.VMEM((2, CHUNK, n_heads * LANES), F32),
            pltpu.VMEM((n_heads, LANES, LANES), F32)]


def _hgrn2_kernel(layer, z_ref, w_ref, lbl_ref, nw_ref, dmat_ref, masks_ref, y_ref,
                  proj_ref, d_ref, e_ref, elast_ref, o_ref, state_ref, key_ref):
    @pl.when(pl.program_id(1) == 0)
    def _():
        state_ref[...] = jnp.zeros_like(state_ref)

    proj_ref[...] = _dot(z_ref[0], w_ref[...])

    logits = [lbl_ref[i:i + 1, :] for i in range(DEPTH)]
    mx = functools.reduce(jnp.maximum, logits)
    ex = [jnp.exp(t - mx) for t in logits]
    tot = functools.reduce(lambda a, b: a + b, ex)
    lb = functools.reduce(lambda a, b: a + b, ex[:layer + 1]) / tot - ex[0] / tot
    nw = nw_ref[...]
    w = BRANCH_W
    ts = proj_ref.shape[0]
    units = [slice(h * LANES, (h + 1) * LANES) for h in range(HG_HEADS)]

    def decay_pieces(ci, slot):
        rows = _chunk_rows(ci)
        zf = proj_ref[rows, w:2 * w]
        en = jnp.exp(-jnp.abs(zf))
        r = 1.0 / (1.0 + en)
        er = en * r
        pos = zf >= 0.0
        sig = jnp.where(pos, r, er)
        nsig = jnp.where(pos, er, r)
        key_ref[slot] = (1.0 - lb) * nsig
        log_f = jnp.log(lb + (1.0 - lb) * sig)
        yield
        yield from _decay_pieces(log_f, dmat_ref, d_ref, e_ref.at[slot], elast_ref.at[slot])

    def score_pieces(ci, slot):
        rows = _chunk_rows(ci)
        q = proj_ref[rows, 0:w] * (HG_DK ** -0.5)
        key = key_ref[slot]
        vin = proj_ref[rows, 2 * w:3 * w]
        yield from _score_pieces([q[:, u] for u in units], [key[:, u] for u in units],
                                 [vin[:, u] for u in units], 1, masks_ref, e_ref.at[slot],
                                 elast_ref.at[slot], state_ref, o_ref.at[slot])

    def tail_pieces(ci, slot):
        rows = _chunk_rows(ci)
        for h, u in enumerate(units):
            gate = proj_ref[rows, 3 * w + h * LANES:3 * w + (h + 1) * LANES]
            y_ref[0, rows, u] = _head_norm_gate(o_ref[slot, :, u], nw, gate).astype(BF16)
            yield

    _pipelined_chunks(ts // CHUNK, decay_pieces, score_pieces, tail_pieces)


def _hgrn2(z, w_all, lb_logits, norm_w, dmat, masks, layer, ts):
    bsz, seqlen, d = z.shape
    n_cols = _w_cols("hgrn2")
    return pl.pallas_call(
        functools.partial(_hgrn2_kernel, layer),
        grid=(bsz, seqlen // ts),
        in_specs=[_row_spec(ts, d), _w_spec("hgrn2", layer, d), _full_spec(lb_logits.shape),
                  _full_spec((1, LANES)), _full_spec(dmat.shape), _full_spec(masks.shape)],
        out_specs=_row_spec(ts, BRANCH_W),
        out_shape=jax.ShapeDtypeStruct((bsz, seqlen, BRANCH_W), BF16),
        scratch_shapes=_attention_scratch(ts, n_cols, BRANCH_W, HG_HEADS)
                       + [pltpu.VMEM((2, CHUNK, BRANCH_W), F32)],
        compiler_params=_params(2),
        name="hgrn2_mixer",
    )(z, w_all, lb_logits, norm_w.reshape(1, LANES), dmat, masks)


def _gla_kernel(z_ref, w_ref, gkw_ref, gkb_ref, nw_ref, dmat_ref, masks_ref, y_ref,
                proj_ref, d_ref, e_ref, elast_ref, o_ref, state_ref):
    @pl.when(pl.program_id(1) == 0)
    def _():
        state_ref[...] = jnp.zeros_like(state_ref)

    proj_ref[...] = _dot(z_ref[0], w_ref[:, 0:proj_ref.shape[1]])
    nw = nw_ref[...]
    gkb = gkb_ref[...]
    qk_w = GLA_HEADS * GLA_DK
    w = BRANCH_W
    ts = proj_ref.shape[0]
    units = [slice(u * LANES, (u + 1) * LANES) for u in range(qk_w // LANES)]
    heads = [slice(h * LANES, (h + 1) * LANES) for h in range(GLA_HEADS)]
    gate_off = 2 * qk_w + w

    def decay_pieces(ci, slot):
        rows = _chunk_rows(ci)
        lowrank = proj_ref[rows, gate_off + w:gate_off + w + LANES]
        gk = _dot(lowrank.astype(BF16), gkw_ref[...]) + gkb
        log_alpha = (jnp.minimum(gk, 0.0) - jnp.log(1.0 + jnp.exp(-jnp.abs(gk)))) * (1.0 / GLA_TAU)
        yield
        yield from _decay_pieces(log_alpha, dmat_ref, d_ref, e_ref.at[slot], elast_ref.at[slot])

    def score_pieces(ci, slot):
        rows = _chunk_rows(ci)
        q = proj_ref[rows, 0:qk_w] * (GLA_DK ** -0.5)
        k = proj_ref[rows, qk_w:2 * qk_w]
        v = proj_ref[rows, 2 * qk_w:2 * qk_w + w]
        yield from _score_pieces([q[:, u] for u in units], [k[:, u] for u in units],
                                 [v[:, hs] for hs in heads], LANES // GLA_DK, masks_ref,
                                 e_ref.at[slot], elast_ref.at[slot], state_ref, o_ref.at[slot])

    def tail_pieces(ci, slot):
        rows = _chunk_rows(ci)
        for h, hs in enumerate(heads):
            gate = proj_ref[rows, gate_off + h * LANES:gate_off + (h + 1) * LANES]
            y_ref[0, rows, hs] = _head_norm_gate(o_ref[slot, :, hs], nw, gate).astype(BF16)
            yield

    _pipelined_chunks(ts // CHUNK, decay_pieces, score_pieces, tail_pieces)


def _gla(z, w_all, gk_w, gk_b, norm_w, dmat, masks, layer, ts):
    bsz, seqlen, d = z.shape
    n_cols = _w_cols("gla")
    qk_w = GLA_HEADS * GLA_DK
    return pl.pallas_call(
        _gla_kernel,
        grid=(bsz, seqlen // ts),
        in_specs=[_row_spec(ts, d), _w_spec("gla", layer, d), _full_spec((LANES, qk_w)),
                  _full_spec((1, qk_w)), _full_spec((1, LANES)), _full_spec(dmat.shape),
                  _full_spec(masks.shape)],
        out_specs=_row_spec(ts, BRANCH_W),
        out_shape=jax.ShapeDtypeStruct((bsz, seqlen, BRANCH_W), BF16),
        scratch_shapes=_attention_scratch(ts, n_cols, qk_w, GLA_HEADS),
        compiler_params=_params(2),
        name="gla_mixer",
    )(z, w_all, gk_w, gk_b.reshape(1, qk_w), norm_w.reshape(1, LANES), dmat, masks)


def _gelu(x):
    return 0.5 * x * (1.0 + lax.erf(x * (2.0 ** -0.5)))


def _sg_kernel(z_ref, w_ref, lnw_ref, lnb_ref, ws_ref, bias_ref, y_ref):
    w = BRANCH_W
    proj = _dot(z_ref[0], w_ref[:, 0:3 * w])
    ts = proj.shape[0]
    u = _gelu(proj[:, 0:w])
    vf = _gelu(proj[:, w:2 * w])
    gate = proj[:, 2 * w:3 * w]
    mu = jnp.mean(vf, axis=-1, keepdims=True)
    dv = vf - mu
    var = jnp.mean(dv * dv, axis=-1, keepdims=True)
    vn = (dv * lax.rsqrt(var + NORM_EPS) * lnw_ref[...] + lnb_ref[...]).astype(BF16)
    ri = lax.broadcasted_iota(jnp.int32, (SG_LEN, SG_LEN), 0)
    cj = lax.broadcasted_iota(jnp.int32, (SG_LEN, SG_LEN), 1)
    allowed = (cj < CHUNK) | (ri >= CHUNK)
    wm = [jnp.where(allowed, ws_ref[g], 0.0).astype(BF16) for g in range(SG_GROUPS)]
    bias = bias_ref[...]
    ug = u * _silu(gate)
    for blk in range(ts // SG_LEN):
        rows = slice(blk * SG_LEN, (blk + 1) * SG_LEN)
        for g in range(SG_GROUPS):
            cols = slice(g * LANES, (g + 1) * LANES)
            mixed = _dot(wm[g], vn[rows, cols]) + bias[:, cols]
            y_ref[0, rows, cols] = (ug[rows, cols] * mixed).astype(BF16)


def _spatial_gating(z, w_all, ln_w, ln_b, w_s, bias, layer, ts):
    bsz, seqlen, d = z.shape
    assert _w_cols("sg") == 3 * BRANCH_W
    return pl.pallas_call(
        _sg_kernel,
        grid=(bsz, seqlen // ts),
        in_specs=[_row_spec(ts, d), _w_spec("sg", layer, d), _full_spec((1, BRANCH_W)),
                  _full_spec((1, BRANCH_W)), _full_spec(w_s.shape), _full_spec(bias.shape)],
        out_specs=_row_spec(ts, BRANCH_W),
        out_shape=jax.ShapeDtypeStruct((bsz, seqlen, BRANCH_W), BF16),
        compiler_params=_params(2),
        name="spatial_gating_mixer",
    )(z, w_all, ln_w.reshape(1, BRANCH_W), ln_b.reshape(1, BRANCH_W), w_s, bias)


def _expm1_nonpos(x):
    p = 1.0 / 720
    for coeff in (1.0 / 120, 1.0 / 24, 1.0 / 6, 0.5, 1.0):
        p = p * x + coeff
    return jnp.where(x > -0.125, p * x, jnp.exp(x) - 1.0)


def _lru_kernel(z_ref, w_ref, cw_ref, cb_ref, wax_ref, ba_ref, bx_ref, lam_ref, y_ref,
                xtb_ref, hs_ref, hprev_ref):
    bsz, tt, d = z_ref.shape
    w = BRANCH_W
    n_rows = bsz * tt
    n_slabs = w // LANES
    tail = (CONV_WIDTH - 1) * bsz

    @pl.when(pl.program_id(0) == 0)
    def _():
        xtb_ref[:, 0:tail, :] = jnp.zeros((n_slabs, tail, LANES), F32)
        hprev_ref[...] = jnp.zeros_like(hprev_ref)

    proj = _dot(z_ref[...].reshape(n_rows, d), w_ref[...])
    gate = proj[:, w:2 * w]
    for b in range(bsz):
        rows = slice(b * tt, (b + 1) * tt)
        for j in range(n_slabs):
            xtb_ref[j, pl.ds(tail + b, tt, stride=bsz), :] = proj[rows, j * LANES:(j + 1) * LANES]

    def delayed(back):
        start = tail - back * bsz
        return jnp.concatenate(
            [xtb_ref[j, start:start + n_rows, :] for j in range(n_slabs)], axis=1)

    xc = cb_ref[...]
    for tap in range(CONV_WIDTH):
        xc = xc + delayed(CONV_WIDTH - 1 - tap) * cw_ref[tap:tap + 1, :]
    for j in range(n_slabs):
        xtb_ref[j, 0:tail, :] = xtb_ref[j, n_rows:n_rows + tail, :]

    xc16 = xc.astype(BF16)
    ra, ix = [], []
    for h in range(LRU_HEADS):
        cols = slice(h * LRU_HD, (h + 1) * LRU_HD)
        both = _dot(xc16[:, cols], wax_ref[h])
        ra.append(both[:, 0:LRU_HD])
        ix.append(both[:, LRU_HD:2 * LRU_HD])
    ig = _sigmoid(jnp.concatenate(ix, axis=1) + bx_ref[...])
    nlam = -lam_ref[...]
    softplus = jnp.maximum(nlam, 0.0) + jnp.log(1.0 + jnp.exp(-jnp.abs(nlam)))
    half_scale = (-0.5 * RG_C) * softplus
    log_a = half_scale + half_scale * jnp.tanh(0.5 * (jnp.concatenate(ra, axis=1) + ba_ref[...]))
    a = jnp.exp(log_a)
    var = jnp.maximum(-_expm1_nonpos(2.0 * log_a), SQRT_EPS)
    bt = (var * lax.rsqrt(var)) * (ig * xc)

    hcur = [hprev_ref[j] for j in range(n_slabs)]
    for t in range(tt):
        step = slice(t * bsz, (t + 1) * bsz)
        for j in range(n_slabs):
            cols = slice(j * LANES, (j + 1) * LANES)
            hcur[j] = a[step, cols] * hcur[j] + bt[step, cols]
            hs_ref[j, step, :] = hcur[j]
    for j in range(n_slabs):
        hprev_ref[j] = hcur[j]

    hseq = jnp.concatenate(
        [jnp.concatenate([hs_ref[j, pl.ds(b, tt, stride=bsz), :] for j in range(n_slabs)], axis=1)
         for b in range(bsz)], axis=0)
    y_ref[...] = (hseq * _silu(gate)).astype(BF16).reshape(bsz, tt, w)


def _rglru(z, w_all, conv_w, conv_b, wax, b_a, b_x, lam, layer, tt):
    bsz, seqlen, d = z.shape
    w = BRANCH_W
    assert _w_cols("lru") == 2 * w
    assert bsz == SUBLANES, "the RG-LRU scan puts the batch on the vreg sublanes"
    n_slabs = w // LANES
    return pl.pallas_call(
        _lru_kernel,
        grid=(seqlen // tt,),
        in_specs=[pl.BlockSpec((bsz, tt, d), lambda i: (0, i, 0)), _w_spec("lru", layer, d),
                  _full_spec((CONV_WIDTH, w)), _full_spec((1, w)), _full_spec(wax.shape),
                  _full_spec((1, w)), _full_spec((1, w)), _full_spec((1, w))],
        out_specs=pl.BlockSpec((bsz, tt, w), lambda i: (0, i, 0)),
        out_shape=jax.ShapeDtypeStruct((bsz, seqlen, w), BF16),
        scratch_shapes=[pltpu.VMEM((n_slabs, bsz * (tt + CONV_WIDTH - 1), LANES), F32),
                        pltpu.VMEM((n_slabs, bsz * tt, LANES), F32),
                        pltpu.VMEM((n_slabs, bsz, LANES), F32)],
        compiler_params=_params(1),
        name="rglru_mixer",
    )(z, w_all, conv_w, conv_b.reshape(1, w), wax, b_a.reshape(1, w), b_x.reshape(1, w),
      lam.reshape(1, w))


MERGE_COLS = 256


def _merge_kernel(final, z_ref, h_ref, ya_ref, yb_ref, yc_ref, yd_ref, wmg_ref, wbr_ref,
                  wout_ref, nw_ref, *rest):
    if final:
        out_ref, merged_ref = rest
    else:
        hn_ref, zn_ref, merged_ref = rest
    zt = z_ref[0]
    ys = [ya_ref[0], yb_ref[0], yc_ref[0], yd_ref[0]]
    for n in range(D_MODEL // MERGE_COLS):
        cols = slice(n * MERGE_COLS, (n + 1) * MERGE_COLS)
        merged = None
        for b in range(N_BRANCH):
            gcols = slice(b * D_MODEL + n * MERGE_COLS, b * D_MODEL + (n + 1) * MERGE_COLS)
            gate = _sigmoid(_dot(zt, wmg_ref[:, gcols]))
            term = gate * _dot(ys[b], wbr_ref[b, :, cols])
            merged = term if merged is None else merged + term
        merged_ref[:, cols] = merged.astype(BF16)
    out = h_ref[0] + _dot(merged_ref[...], wout_ref[...])
    normed = _rms_norm_rows(out, nw_ref[...])
    if final:
        out_ref[0] = normed
    else:
        hn_ref[0] = out
        zn_ref[0] = normed.astype(BF16)


def _merge(z, h, ys, w_all, w_br, w_out, next_norm_w, layer, final, ts):
    bsz, seqlen, d = h.shape
    assert _w_cols("merge") == N_BRANCH * d
    f32_out = jax.ShapeDtypeStruct((bsz, seqlen, d), F32)
    if final:
        out_shape, out_specs = f32_out, _row_spec(ts, d)
    else:
        out_shape = (f32_out, jax.ShapeDtypeStruct((bsz, seqlen, d), BF16))
        out_specs = (_row_spec(ts, d), _row_spec(ts, d))
    return pl.pallas_call(
        functools.partial(_merge_kernel, final),
        grid=(bsz, seqlen // ts),
        in_specs=[_row_spec(ts, d), _row_spec(ts, d)] + [_row_spec(ts, BRANCH_W)] * N_BRANCH
                 + [_w_spec("merge", layer, d),
                    pl.BlockSpec((None,) + w_br.shape[1:], lambda *_: (layer, 0, 0, 0),
                                 pipeline_mode=pl.Buffered(1)),
                    pl.BlockSpec((None,) + w_out.shape[1:], lambda *_: (layer, 0, 0),
                                 pipeline_mode=pl.Buffered(1)),
                    _full_spec((1, d))],
        out_specs=out_specs,
        out_shape=out_shape,
        scratch_shapes=[pltpu.VMEM((ts, d), BF16)],
        compiler_params=_params(2),
        name="merge_final" if final else "merge",
    )(z, h, *ys, w_all, w_br, w_out, next_norm_w.reshape(1, d))


ROWS_PACK = 128
TS_NORM = 1024
TS_ATTN = 1024
TS_SG = 512
TT_LRU = 64
TS_MERGE = 512


def kernel(x, norm_w, w_in, hg_lb_logits, hg_norm_w, gla_gk_w, gla_gk_b, gla_norm_w, sg_ln_w, sg_ln_b, sg_w, sg_b, lru_conv_w, lru_conv_b, lru_w_a, lru_b_a, lru_w_x, lru_b_x, lru_lambda, w_branch, w_out, final_norm_w):
    dmat = jnp.asarray(_decay_matrix(), BF16)
    masks = jnp.asarray(_level_masks(), F32)
    gkw_pad = jnp.zeros((LANES - GLA_RANK, GLA_HEADS * GLA_DK), BF16)
    w_all = _pack_in_proj(w_in, ROWS_PACK)
    w_br = w_branch.astype(BF16)
    w_o = w_out.astype(BF16)

    z = _first_norm(x, norm_w[0], TS_NORM)
    h = x
    for l in range(DEPTH):
        gkw = jnp.concatenate([gla_gk_w[l].astype(BF16), gkw_pad], axis=0)
        wax = jnp.concatenate([lru_w_a[l], lru_w_x[l]], axis=-1).astype(BF16)
        sg_bias = jnp.repeat(sg_b[l].T, LANES, axis=1)

        y_a = _hgrn2(z, w_all, hg_lb_logits, hg_norm_w[l], dmat, masks, l, TS_ATTN)
        y_b = _gla(z, w_all, gkw, gla_gk_b[l], gla_norm_w[l], dmat, masks, l, TS_ATTN)
        y_c = _spatial_gating(z, w_all, sg_ln_w[l], sg_ln_b[l], sg_w[l], sg_bias, l, TS_SG)
        y_d = _rglru(z, w_all, lru_conv_w[l], lru_conv_b[l], wax, lru_b_a[l], lru_b_x[l],
                     lru_lambda[l], l, TT_LRU)
        ys = (y_a, y_b, y_c, y_d)
        if l + 1 < DEPTH:
            h, z = _merge(z, h, ys, w_all, w_br, w_o, norm_w[l + 1], l, False, TS_MERGE)
        else:
            return _merge(z, h, ys, w_all, w_br, w_o, final_norm_w, l, True, TS_MERGE)
```

```python
import functools

import numpy as np
import jax
import jax.numpy as jnp
from jax import lax
from jax.experimental import pallas as pl
from jax.experimental.pallas import tpu as pltpu

F32 = jnp.float32
BF16 = jnp.bfloat16

D_MODEL = 1024
DEPTH = 2
CHUNK = 64
BRANCH_W = 512
N_BRANCH = 4
NORM_EPS = 1e-6
SQRT_EPS = 1e-12
HG_HEADS = 4
HG_DK = 128
GLA_HEADS = 4
GLA_DK = 64
GLA_RANK = 16
GLA_TAU = 16.0
SG_GROUPS = 4
SG_LEN = 128
LRU_HEADS = 4
LRU_HD = 128
CONV_WIDTH = 4
RG_C = 8.0

LANES = 128
SUBLANES = 8
VMEM_LIMIT_BYTES = 56 * 1024 * 1024
LOG2E = 1.4426950408889634

OFF_A = 0
OFF_B = 2048
OFF_LR = 3584
OFF_C = 3600
OFF_D = 5136
OFF_MG = 6160
N_IN = 10256

LEVELS = (32, 16, 8, 4, 2)
N_DECAY_BLOCKS = 2 + len(LEVELS)


def _decay_matrix():
    c = CHUNK
    i = np.arange(c)[:, None]
    j = np.arange(c)[None, :]
    blocks = [(j <= i), (j > i)]
    for s in LEVELS:
        ref = (i // (2 * s)) * (2 * s) + s - 1
        in_b = i > ref
        blocks.append(np.where(in_b, (j > ref) & (j <= i), (j > i) & (j <= ref)))
    m = np.concatenate(blocks, axis=0).astype(np.float32)
    return np.concatenate([m, m, m], axis=1)


def _level_masks():
    c = CHUNK
    i = np.arange(c)[:, None]
    j = np.arange(c)[None, :]
    masks = []
    for s in LEVELS:
        same = (i // (2 * s)) == (j // (2 * s))
        masks.append(same & ((i // s) % 2 == 1) & ((j // s) % 2 == 0))
    return np.stack(masks).astype(np.float32)


def _sigmoid(x):
    return 0.5 + 0.5 * jnp.tanh(0.5 * x)


def _silu(x):
    half = 0.5 * x
    return half + half * jnp.tanh(half)


def _dot(a, b):
    return jnp.dot(a, b, preferred_element_type=F32)


def _dot_nt(a, b):
    return lax.dot_general(a, b, (((1,), (1,)), ((), ())), preferred_element_type=F32)


def _dot_tn(a, b):
    return lax.dot_general(a, b, (((0,), (0,)), ((), ())), preferred_element_type=F32)


def _split3(x):
    hi = x.astype(BF16)
    r1 = x - hi.astype(F32)
    mid = r1.astype(BF16)
    lo = (r1 - mid.astype(F32)).astype(BF16)
    return jnp.concatenate([hi, mid, lo], axis=0)


def _rms_norm_rows(x, w):
    return x * lax.rsqrt(jnp.mean(x * x, axis=-1, keepdims=True) + NORM_EPS) * w


def _params(n_grid_dims):
    return pltpu.CompilerParams(
        dimension_semantics=("arbitrary",) * n_grid_dims,
        vmem_limit_bytes=VMEM_LIMIT_BYTES,
    )


def _full_spec(shape):
    return pl.BlockSpec(shape, lambda *_: (0,) * len(shape))


def _row_spec(ts, width):
    return pl.BlockSpec((1, ts, width), lambda b, s: (b, s, 0))


_W_SEGMENTS = (("hgrn2", 2048, OFF_B - OFF_A), ("gla", 2048, OFF_LR - OFF_B + LANES),
               ("merge", 4096, N_IN - OFF_MG), ("sg", 2048, OFF_D - OFF_C),
               ("lru", 1024, OFF_MG - OFF_D))
_W_SOURCE = {"hgrn2": (OFF_A, OFF_B), "gla": (OFF_B, OFF_C), "merge": (OFF_MG, N_IN),
             "sg": (OFF_C, OFF_D), "lru": (OFF_D, OFF_MG)}


def _pack_in_proj(w_in):
    w16 = jnp.swapaxes(w_in, 1, 2).astype(BF16)
    parts = []
    for name, height, _ in _W_SEGMENTS:
        lo, hi = _W_SOURCE[name]
        parts.append(w16[:, lo:hi, :])
        if height > hi - lo:
            parts.append(jnp.zeros((w16.shape[0], height - (hi - lo), w16.shape[2]), BF16))
    return jnp.concatenate(parts, axis=1)


def _w_spec(name, layer, d):
    offset = 0
    for seg, height, _ in _W_SEGMENTS:
        if seg == name:
            assert offset % height == 0
            return pl.BlockSpec((None, height, d), lambda *_: (layer, offset // height, 0),
                                pipeline_mode=pl.Buffered(1))
        offset += height
    raise KeyError(name)


def _w_cols(name):
    return {seg: used for seg, _, used in _W_SEGMENTS}[name]


def _norm_kernel(x_ref, w_ref, z_ref):
    z_ref[0] = _rms_norm_rows(x_ref[0], w_ref[...]).astype(BF16)


def _first_norm(x, w, ts):
    bsz, seqlen, d = x.shape
    return pl.pallas_call(
        _norm_kernel,
        grid=(bsz, seqlen // ts),
        in_specs=[_row_spec(ts, d), _full_spec((1, d))],
        out_specs=_row_spec(ts, d),
        out_shape=jax.ShapeDtypeStruct((bsz, seqlen, d), BF16),
        compiler_params=_params(2),
        name="first_norm",
    )(x, w.reshape(1, d))


def _interleave(*pieces):
    live = list(pieces)
    while live:
        for p in list(live):
            try:
                next(p)
            except StopIteration:
                live.remove(p)


def _pipelined_chunks(n_chunks, decay_pieces, score_pieces, tail_pieces):
    assert n_chunks % 2 == 0 and n_chunks >= 4
    _interleave(decay_pieces(0, 0))
    _interleave(decay_pieces(1, 1), score_pieces(0, 0))

    def body(i, carry):
        h = 2 * i + 1
        _interleave(decay_pieces(h + 1, 0), score_pieces(h, 1), tail_pieces(h - 1, 0))
        _interleave(decay_pieces(h + 2, 1), score_pieces(h + 1, 0), tail_pieces(h, 1))
        return carry

    lax.fori_loop(0, (n_chunks - 2) // 2, body, 0)
    _interleave(score_pieces(n_chunks - 1, 1), tail_pieces(n_chunks - 2, 0))
    _interleave(tail_pieces(n_chunks - 1, 1))


def _chunk_rows(ci):
    start = ci * CHUNK
    if not isinstance(ci, int):
        start = pl.multiple_of(start, CHUNK)
    return pl.ds(start, CHUNK)


def _decay_pieces(log_decay, dmat_ref, d_ref, e_ref, elast_ref):
    d_ref[...] = _dot(dmat_ref[...], _split3(log_decay * LOG2E))
    yield
    for blk in range(N_DECAY_BLOCKS):
        rows = slice(blk * CHUNK, (blk + 1) * CHUNK)
        ex = jnp.exp2(d_ref[rows, :])
        e_ref[rows, :] = ex.astype(BF16)
        if blk == 0:
            elast_ref[...] = ex[CHUNK - SUBLANES:CHUNK, :]
        yield


def _score_pieces(q_units, k_units, v_heads, decay_units, heads_per_unit, masks_ref, e_ref,
                  elast_ref, state_ref, o_ref):
    c = CHUNK
    n_heads = len(v_heads)
    lane = lax.broadcasted_iota(jnp.int32, (c, LANES), 1)
    head_w = LANES // heads_per_unit
    unit_lanes = [slice(u * LANES, (u + 1) * LANES) for u in range(len(q_units))]
    k16_units = [k.astype(BF16) for k in k_units]
    qf_heads, q16_heads, kown_heads = [], [], []
    for h in range(n_heads):
        u = h // heads_per_unit
        qf = q_units[u]
        k16 = k16_units[u]
        if heads_per_unit > 1:
            sub = h % heads_per_unit
            in_head = (lane >= sub * head_w) & (lane < (sub + 1) * head_w)
            qf = jnp.where(in_head, qf, 0.0)
            kown_heads.append(jnp.where(in_head, k16, jnp.zeros_like(k16)))
        else:
            kown_heads.append(k16)
        qf_heads.append(qf)
        q16_heads.append(qf.astype(BF16))
    v16_heads = [v.astype(BF16) for v in v_heads]
    yield

    odd_row = jnp.bitwise_and(lax.broadcasted_iota(jnp.int32, (c, 1), 0), 1) == 1
    k_prev_units = [pltpu.roll(k, 1, axis=0) for k in k_units]
    fine = []
    for h in range(n_heads):
        u = h // heads_per_unit
        s_diag = jnp.sum(qf_heads[h] * k_units[u], axis=-1, keepdims=True)
        s_pair = jnp.sum(qf_heads[h] * decay_units[u] * k_prev_units[u], axis=-1, keepdims=True)
        s_pair = jnp.where(odd_row, s_pair, 0.0)
        fine.append(s_diag * v_heads[h] + s_pair * pltpu.roll(v_heads[h], 1, axis=0))
    yield

    scores = [None] * n_heads
    for lvl in range(len(LEVELS)):
        e_units = [e_ref[(2 + lvl) * c:(3 + lvl) * c, ul] for ul in unit_lanes]
        ke_units = [k16_units[u] * e_units[u] for u in range(len(unit_lanes))]
        for h in range(n_heads):
            u = h // heads_per_unit
            term = masks_ref[lvl] * _dot_nt(q16_heads[h] * e_units[u], ke_units[u])
            scores[h] = term if scores[h] is None else scores[h] + term
        yield

    states = [state_ref[h] for h in range(n_heads)]
    for h in range(n_heads):
        e_in = e_ref[0:c, unit_lanes[h // heads_per_unit]]
        o_ref[:, h * LANES:(h + 1) * LANES] = (
            _dot(scores[h].astype(BF16), v16_heads[h])
            + _dot(q16_heads[h] * e_in, states[h].astype(BF16)) + fine[h])
    yield
    for h in range(n_heads):
        lanes = unit_lanes[h // heads_per_unit]
        e_out = e_ref[c:2 * c, lanes]
        e_last = elast_ref[SUBLANES - 1:SUBLANES, lanes]
        e_col = jnp.transpose(jnp.broadcast_to(e_last, (LANES, LANES)))
        state_ref[h] = states[h] * e_col + _dot_tn(kown_heads[h] * e_out, v16_heads[h])
    yield


def _head_norm_gate(o, norm_w, gate):
    y = o * lax.rsqrt(jnp.mean(o * o, axis=-1, keepdims=True) + NORM_EPS) * norm_w
    return y * _silu(gate)


def _attention_scratch(ts, n_cols, decay_w, n_heads):
    return [pltpu.VMEM((ts, n_cols), F32),
            pltpu.VMEM((N_DECAY_BLOCKS * CHUNK, decay_w), F32),
            pltpu.VMEM((2, N_DECAY_BLOCKS * CHUNK, decay_w), BF16),
            pltpu.VMEM((2, SUBLANES, decay_w), F32),
            pltpu.VMEM((2, CHUNK, decay_w), F32),
            pltpu.VMEM((2, CHUNK, n_heads * LANES), F32),
            pltpu.VMEM((n_heads, LANES, LANES), F32)]


def _hgrn2_kernel(layer, z_ref, w_ref, lbl_ref, nw_ref, dmat_ref, masks_ref, y_ref,
                  proj_ref, d_ref, e_ref, elast_ref, dec_ref, o_ref, state_ref, key_ref):
    @pl.when(pl.program_id(1) == 0)
    def _():
        state_ref[...] = jnp.zeros_like(state_ref)

    proj_ref[...] = _dot_nt(z_ref[0], w_ref[...])

    logits = [lbl_ref[i:i + 1, :] for i in range(DEPTH)]
    mx = functools.reduce(jnp.maximum, logits)
    ex = [jnp.exp(t - mx) for t in logits]
    tot = functools.reduce(lambda a, b: a + b, ex)
    lb = functools.reduce(lambda a, b: a + b, ex[:layer + 1]) / tot - ex[0] / tot
    nw = nw_ref[...]
    w = BRANCH_W
    ts = proj_ref.shape[0]
    units = [slice(h * LANES, (h + 1) * LANES) for h in range(HG_HEADS)]

    def decay_pieces(ci, slot):
        rows = _chunk_rows(ci)
        zf = proj_ref[rows, w:2 * w]
        en = jnp.exp(-jnp.abs(zf))
        r = 1.0 / (1.0 + en)
        er = en * r
        pos = zf >= 0.0
        sig = jnp.where(pos, r, er)
        nsig = jnp.where(pos, er, r)
        key_ref[slot] = (1.0 - lb) * nsig
        forget = lb + (1.0 - lb) * sig
        dec_ref[slot] = forget
        log_f = jnp.log(forget)
        yield
        yield from _decay_pieces(log_f, dmat_ref, d_ref, e_ref.at[slot], elast_ref.at[slot])

    def score_pieces(ci, slot):
        rows = _chunk_rows(ci)
        q = proj_ref[rows, 0:w] * (HG_DK ** -0.5)
        key = key_ref[slot]
        forget = dec_ref[slot]
        vin = proj_ref[rows, 2 * w:3 * w]
        yield from _score_pieces([q[:, u] for u in units], [key[:, u] for u in units],
                                 [vin[:, u] for u in units], [forget[:, u] for u in units], 1,
                                 masks_ref, e_ref.at[slot], elast_ref.at[slot], state_ref,
                                 o_ref.at[slot])

    def tail_pieces(ci, slot):
        rows = _chunk_rows(ci)
        for h, u in enumerate(units):
            gate = proj_ref[rows, 3 * w + h * LANES:3 * w + (h + 1) * LANES]
            y_ref[0, rows, u] = _head_norm_gate(o_ref[slot, :, u], nw, gate).astype(BF16)
            yield

    _pipelined_chunks(ts // CHUNK, decay_pieces, score_pieces, tail_pieces)


def _hgrn2(z, w_all, lb_logits, norm_w, dmat, masks, layer, ts):
    bsz, seqlen, d = z.shape
    n_cols = _w_cols("hgrn2")
    return pl.pallas_call(
        functools.partial(_hgrn2_kernel, layer),
        grid=(bsz, seqlen // ts),
        in_specs=[_row_spec(ts, d), _w_spec("hgrn2", layer, d), _full_spec(lb_logits.shape),
                  _full_spec((1, LANES)), _full_spec(dmat.shape), _full_spec(masks.shape)],
        out_specs=_row_spec(ts, BRANCH_W),
        out_shape=jax.ShapeDtypeStruct((bsz, seqlen, BRANCH_W), BF16),
        scratch_shapes=_attention_scratch(ts, n_cols, BRANCH_W, HG_HEADS)
                       + [pltpu.VMEM((2, CHUNK, BRANCH_W), F32)],
        compiler_params=_params(2),
        name="hgrn2_mixer",
    )(z, w_all, lb_logits, norm_w.reshape(1, LANES), dmat, masks)


def _gla_kernel(z_ref, w_ref, gkw_ref, gkb_ref, nw_ref, dmat_ref, masks_ref, y_ref,
                proj_ref, d_ref, e_ref, elast_ref, dec_ref, o_ref, state_ref):
    @pl.when(pl.program_id(1) == 0)
    def _():
        state_ref[...] = jnp.zeros_like(state_ref)

    proj_ref[...] = _dot_nt(z_ref[0], w_ref[0:proj_ref.shape[1], :])
    nw = nw_ref[...]
    gkb = gkb_ref[...]
    qk_w = GLA_HEADS * GLA_DK
    w = BRANCH_W
    ts = proj_ref.shape[0]
    units = [slice(u * LANES, (u + 1) * LANES) for u in range(qk_w // LANES)]
    heads = [slice(h * LANES, (h + 1) * LANES) for h in range(GLA_HEADS)]
    gate_off = 2 * qk_w + w

    def decay_pieces(ci, slot):
        rows = _chunk_rows(ci)
        lowrank = proj_ref[rows, gate_off + w:gate_off + w + LANES]
        gk = _dot(lowrank.astype(BF16), gkw_ref[...]) + gkb
        log_alpha = (jnp.minimum(gk, 0.0) - jnp.log(1.0 + jnp.exp(-jnp.abs(gk)))) * (1.0 / GLA_TAU)
        dec_ref[slot] = jnp.exp(log_alpha)
        yield
        yield from _decay_pieces(log_alpha, dmat_ref, d_ref, e_ref.at[slot], elast_ref.at[slot])

    def score_pieces(ci, slot):
        rows = _chunk_rows(ci)
        q = proj_ref[rows, 0:qk_w] * (GLA_DK ** -0.5)
        k = proj_ref[rows, qk_w:2 * qk_w]
        v = proj_ref[rows, 2 * qk_w:2 * qk_w + w]
        alpha = dec_ref[slot]
        yield from _score_pieces([q[:, u] for u in units], [k[:, u] for u in units],
                                 [v[:, hs] for hs in heads], [alpha[:, u] for u in units],
                                 LANES // GLA_DK, masks_ref, e_ref.at[slot], elast_ref.at[slot],
                                 state_ref, o_ref.at[slot])

    def tail_pieces(ci, slot):
        rows = _chunk_rows(ci)
        for h, hs in enumerate(heads):
            gate = proj_ref[rows, gate_off + h * LANES:gate_off + (h + 1) * LANES]
            y_ref[0, rows, hs] = _head_norm_gate(o_ref[slot, :, hs], nw, gate).astype(BF16)
            yield

    _pipelined_chunks(ts // CHUNK, decay_pieces, score_pieces, tail_pieces)


def _gla(z, w_all, gk_w, gk_b, norm_w, dmat, masks, layer, ts):
    bsz, seqlen, d = z.shape
    n_cols = _w_cols("gla")
    qk_w = GLA_HEADS * GLA_DK
    return pl.pallas_call(
        _gla_kernel,
        grid=(bsz, seqlen // ts),
        in_specs=[_row_spec(ts, d), _w_spec("gla", layer, d), _full_spec((LANES, qk_w)),
                  _full_spec((1, qk_w)), _full_spec((1, LANES)), _full_spec(dmat.shape),
                  _full_spec(masks.shape)],
        out_specs=_row_spec(ts, BRANCH_W),
        out_shape=jax.ShapeDtypeStruct((bsz, seqlen, BRANCH_W), BF16),
        scratch_shapes=_attention_scratch(ts, n_cols, qk_w, GLA_HEADS),
        compiler_params=_params(2),
        name="gla_mixer",
    )(z, w_all, gk_w, gk_b.reshape(1, qk_w), norm_w.reshape(1, LANES), dmat, masks)


def _gelu(x):
    return 0.5 * x * (1.0 + lax.erf(x * (2.0 ** -0.5)))


def _sg_kernel(z_ref, w_ref, lnw_ref, lnb_ref, ws_ref, bias_ref, y_ref):
    w = BRANCH_W
    proj = _dot_nt(z_ref[0], w_ref[0:3 * w, :])
    ts = proj.shape[0]
    u = _gelu(proj[:, 0:w])
    vf = _gelu(proj[:, w:2 * w])
    gate = proj[:, 2 * w:3 * w]
    mu = jnp.mean(vf, axis=-1, keepdims=True)
    dv = vf - mu
    var = jnp.mean(dv * dv, axis=-1, keepdims=True)
    vn = (dv * lax.rsqrt(var + NORM_EPS) * lnw_ref[...] + lnb_ref[...]).astype(BF16)
    ri = lax.broadcasted_iota(jnp.int32, (SG_LEN, SG_LEN), 0)
    cj = lax.broadcasted_iota(jnp.int32, (SG_LEN, SG_LEN), 1)
    allowed = (cj < CHUNK) | (ri >= CHUNK)
    wm = [jnp.where(allowed, ws_ref[g], 0.0).astype(BF16) for g in range(SG_GROUPS)]
    bias = bias_ref[...]
    ug = u * _silu(gate)
    for blk in range(ts // SG_LEN):
        rows = slice(blk * SG_LEN, (blk + 1) * SG_LEN)
        for g in range(SG_GROUPS):
            cols = slice(g * LANES, (g + 1) * LANES)
            mixed = _dot(wm[g], vn[rows, cols]) + bias[:, cols]
            y_ref[0, rows, cols] = (ug[rows, cols] * mixed).astype(BF16)


def _spatial_gating(z, w_all, ln_w, ln_b, w_s, bias, layer, ts):
    bsz, seqlen, d = z.shape
    assert _w_cols("sg") == 3 * BRANCH_W
    return pl.pallas_call(
        _sg_kernel,
        grid=(bsz, seqlen // ts),
        in_specs=[_row_spec(ts, d), _w_spec("sg", layer, d), _full_spec((1, BRANCH_W)),
                  _full_spec((1, BRANCH_W)), _full_spec(w_s.shape), _full_spec(bias.shape)],
        out_specs=_row_spec(ts, BRANCH_W),
        out_shape=jax.ShapeDtypeStruct((bsz, seqlen, BRANCH_W), BF16),
        compiler_params=_params(2),
        name="spatial_gating_mixer",
    )(z, w_all, ln_w.reshape(1, BRANCH_W), ln_b.reshape(1, BRANCH_W), w_s, bias)


def _expm1_nonpos(x):
    p = 1.0 / 720
    for coeff in (1.0 / 120, 1.0 / 24, 1.0 / 6, 0.5, 1.0):
        p = p * x + coeff
    return jnp.where(x > -0.125, p * x, jnp.exp(x) - 1.0)


def _lru_kernel(z_ref, w_ref, cw_ref, cb_ref, wax_ref, ba_ref, bx_ref, lam_ref, y_ref,
                xtb_ref, hs_ref, hprev_ref):
    bsz, tt, d = z_ref.shape
    w = BRANCH_W
    n_rows = bsz * tt
    n_slabs = w // LANES
    tail = (CONV_WIDTH - 1) * bsz

    @pl.when(pl.program_id(0) == 0)
    def _():
        xtb_ref[:, 0:tail, :] = jnp.zeros((n_slabs, tail, LANES), F32)
        hprev_ref[...] = jnp.zeros_like(hprev_ref)

    proj = _dot_nt(z_ref[...].reshape(n_rows, d), w_ref[...])
    gate = proj[:, w:2 * w]
    for b in range(bsz):
        rows = slice(b * tt, (b + 1) * tt)
        for j in range(n_slabs):
            xtb_ref[j, pl.ds(tail + b, tt, stride=bsz), :] = proj[rows, j * LANES:(j + 1) * LANES]

    def delayed(back):
        start = tail - back * bsz
        return jnp.concatenate(
            [xtb_ref[j, start:start + n_rows, :] for j in range(n_slabs)], axis=1)

    xc = cb_ref[...]
    for tap in range(CONV_WIDTH):
        xc = xc + delayed(CONV_WIDTH - 1 - tap) * cw_ref[tap:tap + 1, :]
    for j in range(n_slabs):
        xtb_ref[j, 0:tail, :] = xtb_ref[j, n_rows:n_rows + tail, :]

    xc16 = xc.astype(BF16)
    ra, ix = [], []
    for h in range(LRU_HEADS):
        cols = slice(h * LRU_HD, (h + 1) * LRU_HD)
        both = _dot(xc16[:, cols], wax_ref[h])
        ra.append(both[:, 0:LRU_HD])
        ix.append(both[:, LRU_HD:2 * LRU_HD])
    ig = _sigmoid(jnp.concatenate(ix, axis=1) + bx_ref[...])
    nlam = -lam_ref[...]
    softplus = jnp.maximum(nlam, 0.0) + jnp.log(1.0 + jnp.exp(-jnp.abs(nlam)))
    half_scale = (-0.5 * RG_C) * softplus
    log_a = half_scale + half_scale * jnp.tanh(0.5 * (jnp.concatenate(ra, axis=1) + ba_ref[...]))
    a = jnp.exp(log_a)
    var = jnp.maximum(-_expm1_nonpos(2.0 * log_a), SQRT_EPS)
    bt = (var * lax.rsqrt(var)) * (ig * xc)

    hcur = [hprev_ref[j] for j in range(n_slabs)]
    for t in range(tt):
        step = slice(t * bsz, (t + 1) * bsz)
        for j in range(n_slabs):
            cols = slice(j * LANES, (j + 1) * LANES)
            hcur[j] = a[step, cols] * hcur[j] + bt[step, cols]
            hs_ref[j, step, :] = hcur[j]
    for j in range(n_slabs):
        hprev_ref[j] = hcur[j]

    hseq = jnp.concatenate(
        [jnp.concatenate([hs_ref[j, pl.ds(b, tt, stride=bsz), :] for j in range(n_slabs)], axis=1)
         for b in range(bsz)], axis=0)
    y_ref[...] = (hseq * _silu(gate)).astype(BF16).reshape(bsz, tt, w)


def _rglru(z, w_all, conv_w, conv_b, wax, b_a, b_x, lam, layer, tt):
    bsz, seqlen, d = z.shape
    w = BRANCH_W
    assert _w_cols("lru") == 2 * w
    assert bsz == SUBLANES, "the RG-LRU scan puts the batch on the vreg sublanes"
    n_slabs = w // LANES
    return pl.pallas_call(
        _lru_kernel,
        grid=(seqlen // tt,),
        in_specs=[pl.BlockSpec((bsz, tt, d), lambda i: (0, i, 0)), _w_spec("lru", layer, d),
                  _full_spec((CONV_WIDTH, w)), _full_spec((1, w)), _full_spec(wax.shape),
                  _full_spec((1, w)), _full_spec((1, w)), _full_spec((1, w))],
        out_specs=pl.BlockSpec((bsz, tt, w), lambda i: (0, i, 0)),
        out_shape=jax.ShapeDtypeStruct((bsz, seqlen, w), BF16),
        scratch_shapes=[pltpu.VMEM((n_slabs, bsz * (tt + CONV_WIDTH - 1), LANES), F32),
                        pltpu.VMEM((n_slabs, bsz * tt, LANES), F32),
                        pltpu.VMEM((n_slabs, bsz, LANES), F32)],
        compiler_params=_params(1),
        name="rglru_mixer",
    )(z, w_all, conv_w, conv_b.reshape(1, w), wax, b_a.reshape(1, w), b_x.reshape(1, w),
      lam.reshape(1, w))


MERGE_COLS = 256


def _merge_kernel(final, z_ref, h_ref, ya_ref, yb_ref, yc_ref, yd_ref, wmg_ref, wbr_ref,
                  wout_ref, nw_ref, *rest):
    if final:
        out_ref, merged_ref = rest
    else:
        hn_ref, zn_ref, merged_ref = rest
    zt = z_ref[0]
    ys = [ya_ref[0], yb_ref[0], yc_ref[0], yd_ref[0]]
    for n in range(D_MODEL // MERGE_COLS):
        cols = slice(n * MERGE_COLS, (n + 1) * MERGE_COLS)
        merged = None
        for b in range(N_BRANCH):
            gcols = slice(b * D_MODEL + n * MERGE_COLS, b * D_MODEL + (n + 1) * MERGE_COLS)
            gate = _sigmoid(_dot_nt(zt, wmg_ref[gcols, :]))
            term = gate * _dot(ys[b], wbr_ref[b, :, cols])
            merged = term if merged is None else merged + term
        merged_ref[:, cols] = merged.astype(BF16)
    out = h_ref[0] + _dot(merged_ref[...], wout_ref[...])
    normed = _rms_norm_rows(out, nw_ref[...])
    if final:
        out_ref[0] = normed
    else:
        hn_ref[0] = out
        zn_ref[0] = normed.astype(BF16)


def _merge(z, h, ys, w_all, w_br, w_out, next_norm_w, layer, final, ts):
    bsz, seqlen, d = h.shape
    assert _w_cols("merge") == N_BRANCH * d
    f32_out = jax.ShapeDtypeStruct((bsz, seqlen, d), F32)
    if final:
        out_shape, out_specs = f32_out, _row_spec(ts, d)
    else:
        out_shape = (f32_out, jax.ShapeDtypeStruct((bsz, seqlen, d), BF16))
        out_specs = (_row_spec(ts, d), _row_spec(ts, d))
    return pl.pallas_call(
        functools.partial(_merge_kernel, final),
        grid=(bsz, seqlen // ts),
        in_specs=[_row_spec(ts, d), _row_spec(ts, d)] + [_row_spec(ts, BRANCH_W)] * N_BRANCH
                 + [_w_spec("merge", layer, d),
                    pl.BlockSpec((None,) + w_br.shape[1:], lambda *_: (layer, 0, 0, 0),
                                 pipeline_mode=pl.Buffered(1)),
                    pl.BlockSpec((None,) + w_out.shape[1:], lambda *_: (layer, 0, 0),
                                 pipeline_mode=pl.Buffered(1)),
                    _full_spec((1, d))],
        out_specs=out_specs,
        out_shape=out_shape,
        scratch_shapes=[pltpu.VMEM((ts, d), BF16)],
        compiler_params=_params(2),
        name="merge_final" if final else "merge",
    )(z, h, *ys, w_all, w_br, w_out, next_norm_w.reshape(1, d))


TS_NORM = 1024
TS_ATTN = 1024
TS_SG = 512
TT_LRU = 64
TS_MERGE = 512


def kernel(x, norm_w, w_in, hg_lb_logits, hg_norm_w, gla_gk_w, gla_gk_b, gla_norm_w, sg_ln_w, sg_ln_b, sg_w, sg_b, lru_conv_w, lru_conv_b, lru_w_a, lru_b_a, lru_w_x, lru_b_x, lru_lambda, w_branch, w_out, final_norm_w):
    dmat = jnp.asarray(_decay_matrix(), BF16)
    masks = jnp.asarray(_level_masks(), F32)
    gkw_pad = jnp.zeros((LANES - GLA_RANK, GLA_HEADS * GLA_DK), BF16)
    w_all = _pack_in_proj(w_in)
    w_br = w_branch.astype(BF16)
    w_o = w_out.astype(BF16)

    z = _first_norm(x, norm_w[0], TS_NORM)
    h = x
    for l in range(DEPTH):
        gkw = jnp.concatenate([gla_gk_w[l].astype(BF16), gkw_pad], axis=0)
        wax = jnp.concatenate([lru_w_a[l], lru_w_x[l]], axis=-1).astype(BF16)
        sg_bias = jnp.repeat(sg_b[l].T, LANES, axis=1)

        y_a = _hgrn2(z, w_all, hg_lb_logits, hg_norm_w[l], dmat, masks, l, TS_ATTN)
        y_b = _gla(z, w_all, gkw, gla_gk_b[l], gla_norm_w[l], dmat, masks, l, TS_ATTN)
        y_c = _spatial_gating(z, w_all, sg_ln_w[l], sg_ln_b[l], sg_w[l], sg_bias, l, TS_SG)
        y_d = _rglru(z, w_all, lru_conv_w[l], lru_conv_b[l], wax, lru_b_a[l], lru_b_x[l],
                     lru_lambda[l], l, TT_LRU)
        ys = (y_a, y_b, y_c, y_d)
        if l + 1 < DEPTH:
            h, z = _merge(z, h, ys, w_all, w_br, w_o, norm_w[l + 1], l, False, TS_MERGE)
        else:
            return _merge(z, h, ys, w_all, w_br, w_o, final_norm_w, l, True, TS_MERGE)
```

```python
import functools

import numpy as np
import jax
import jax.numpy as jnp
from jax import lax
from jax.experimental import pallas as pl
from jax.experimental.pallas import tpu as pltpu

F32 = jnp.float32
BF16 = jnp.bfloat16

D_MODEL = 1024
DEPTH = 2
CHUNK = 64
BRANCH_W = 512
N_BRANCH = 4
NORM_EPS = 1e-6
SQRT_EPS = 1e-12
HG_HEADS = 4
HG_DK = 128
GLA_HEADS = 4
GLA_DK = 64
GLA_RANK = 16
GLA_TAU = 16.0
SG_GROUPS = 4
SG_LEN = 128
LRU_HEADS = 4
LRU_HD = 128
CONV_WIDTH = 4
RG_C = 8.0

LANES = 128
SUBLANES = 8
VMEM_LIMIT_BYTES = 56 * 1024 * 1024
LOG2E = 1.4426950408889634

OFF_A = 0
OFF_B = 2048
OFF_LR = 3584
OFF_C = 3600
OFF_D = 5136
OFF_MG = 6160
N_IN = 10256

LEVELS = (32, 16, 8, 4, 2)
N_DECAY_BLOCKS = 2 + len(LEVELS)
SPLIT_TERMS = 3


def _decay_matrix():
    c = CHUNK
    i = np.arange(c)[:, None]
    j = np.arange(c)[None, :]
    blocks = [(j <= i), (j > i)]
    for s in LEVELS:
        ref = (i // (2 * s)) * (2 * s) + s - 1
        in_b = i > ref
        blocks.append(np.where(in_b, (j > ref) & (j <= i), (j > i) & (j <= ref)))
    m = np.concatenate(blocks, axis=0).astype(np.float32)
    return np.concatenate([m] * SPLIT_TERMS, axis=1)


def _level_masks():
    c = CHUNK
    i = np.arange(c)[:, None]
    j = np.arange(c)[None, :]
    masks = []
    for s in LEVELS:
        same = (i // (2 * s)) == (j // (2 * s))
        masks.append(same & ((i // s) % 2 == 1) & ((j // s) % 2 == 0))
    return np.stack(masks).astype(np.float32)


def _sigmoid(x):
    return 0.5 + 0.5 * jnp.tanh(0.5 * x)


def _silu(x):
    half = 0.5 * x
    return half + half * jnp.tanh(half)


def _dot(a, b):
    return jnp.dot(a, b, preferred_element_type=F32)


def _dot_nt(a, b):
    return lax.dot_general(a, b, (((1,), (1,)), ((), ())), preferred_element_type=F32)


def _dot_tn(a, b):
    return lax.dot_general(a, b, (((0,), (0,)), ((), ())), preferred_element_type=F32)


def _split_bf16(x):
    terms = []
    rest = x
    for _ in range(SPLIT_TERMS):
        t = rest.astype(BF16)
        terms.append(t)
        rest = rest - t.astype(F32)
    return jnp.concatenate(terms, axis=0)


def _rms_norm_rows(x, w):
    return x * lax.rsqrt(jnp.mean(x * x, axis=-1, keepdims=True) + NORM_EPS) * w


def _params(n_grid_dims):
    return pltpu.CompilerParams(
        dimension_semantics=("arbitrary",) * n_grid_dims,
        vmem_limit_bytes=VMEM_LIMIT_BYTES,
    )


def _full_spec(shape):
    return pl.BlockSpec(shape, lambda *_: (0,) * len(shape))


def _row_spec(ts, width):
    return pl.BlockSpec((1, ts, width), lambda b, s: (b, s, 0))


_W_SEGMENTS = (("hgrn2", 2048, OFF_B - OFF_A), ("gla", 2048, OFF_LR - OFF_B + LANES),
               ("merge", 4096, N_IN - OFF_MG), ("sg", 2048, OFF_D - OFF_C),
               ("lru", 1024, OFF_MG - OFF_D))
_W_SOURCE = {"hgrn2": (OFF_A, OFF_B), "gla": (OFF_B, OFF_C), "merge": (OFF_MG, N_IN),
             "sg": (OFF_C, OFF_D), "lru": (OFF_D, OFF_MG)}


def _pack_in_proj(w_in):
    w16 = jnp.swapaxes(w_in, 1, 2).astype(BF16)
    parts = []
    for name, height, _ in _W_SEGMENTS:
        lo, hi = _W_SOURCE[name]
        parts.append(w16[:, lo:hi, :])
        if height > hi - lo:
            parts.append(jnp.zeros((w16.shape[0], height - (hi - lo), w16.shape[2]), BF16))
    return jnp.concatenate(parts, axis=1)


def _w_spec(name, layer, d):
    offset = 0
    for seg, height, _ in _W_SEGMENTS:
        if seg == name:
            assert offset % height == 0
            return pl.BlockSpec((None, height, d), lambda *_: (layer, offset // height, 0),
                                pipeline_mode=pl.Buffered(1))
        offset += height
    raise KeyError(name)


def _w_cols(name):
    return {seg: used for seg, _, used in _W_SEGMENTS}[name]


def _norm_kernel(x_ref, w_ref, z_ref):
    z_ref[0] = _rms_norm_rows(x_ref[0], w_ref[...]).astype(BF16)


def _first_norm(x, w, ts):
    bsz, seqlen, d = x.shape
    return pl.pallas_call(
        _norm_kernel,
        grid=(bsz, seqlen // ts),
        in_specs=[_row_spec(ts, d), _full_spec((1, d))],
        out_specs=_row_spec(ts, d),
        out_shape=jax.ShapeDtypeStruct((bsz, seqlen, d), BF16),
        compiler_params=_params(2),
        name="first_norm",
    )(x, w.reshape(1, d))


CHUNKS_PER_ITERATION = 2


def _interleave(*pieces):
    live = list(pieces)
    while live:
        for p in list(live):
            try:
                next(p)
            except StopIteration:
                live.remove(p)


def _pipelined_chunks(n_chunks, decay_pieces, score_pieces, tail_pieces):
    assert n_chunks % 2 == 0 and n_chunks >= 4 and CHUNKS_PER_ITERATION % 2 == 0

    def step(c, slot):
        _interleave(score_pieces(c, slot), decay_pieces(c + 1, 1 - slot),
                    tail_pieces(c - 1, 1 - slot))

    _interleave(decay_pieces(0, 0))
    _interleave(decay_pieces(1, 1), score_pieces(0, 0))
    n_loop = (n_chunks - 2) // CHUNKS_PER_ITERATION

    def body(i, carry):
        first = CHUNKS_PER_ITERATION * i + 1
        for k in range(CHUNKS_PER_ITERATION):
            step(first + k, (1 + k) % 2)
        return carry

    lax.fori_loop(0, n_loop, body, 0)
    for c in range(1 + n_loop * CHUNKS_PER_ITERATION, n_chunks - 1):
        step(c, c % 2)
    _interleave(score_pieces(n_chunks - 1, 1), tail_pieces(n_chunks - 2, 0))
    _interleave(tail_pieces(n_chunks - 1, 1))


def _chunk_rows(ci):
    start = ci * CHUNK
    if not isinstance(ci, int):
        start = pl.multiple_of(start, CHUNK)
    return pl.ds(start, CHUNK)


def _decay_pieces(log_decay, dmat_ref, d_ref, e_ref, elast_ref):
    d_ref[...] = _dot(dmat_ref[...], _split_bf16(log_decay * LOG2E))
    yield
    for blk in range(N_DECAY_BLOCKS):
        rows = slice(blk * CHUNK, (blk + 1) * CHUNK)
        ex = jnp.exp2(d_ref[rows, :])
        e_ref[rows, :] = ex.astype(BF16)
        if blk == 0:
            elast_ref[...] = ex[CHUNK - SUBLANES:CHUNK, :]
        yield


def _score_pieces(q_units, k_units, v_heads, decay_units, heads_per_unit, masks_ref, e_ref,
                  elast_ref, state_ref, o_ref):
    c = CHUNK
    n_heads = len(v_heads)
    lane = lax.broadcasted_iota(jnp.int32, (c, LANES), 1)
    head_w = LANES // heads_per_unit
    unit_lanes = [slice(u * LANES, (u + 1) * LANES) for u in range(len(q_units))]
    k16_units = [k.astype(BF16) for k in k_units]
    qf_heads, q16_heads, kown_heads = [], [], []
    for h in range(n_heads):
        u = h // heads_per_unit
        qf = q_units[u]
        k16 = k16_units[u]
        if heads_per_unit > 1:
            sub = h % heads_per_unit
            in_head = (lane >= sub * head_w) & (lane < (sub + 1) * head_w)
            qf = jnp.where(in_head, qf, 0.0)
            kown_heads.append(jnp.where(in_head, k16, jnp.zeros_like(k16)))
        else:
            kown_heads.append(k16)
        qf_heads.append(qf)
        q16_heads.append(qf.astype(BF16))
    v16_heads = [v.astype(BF16) for v in v_heads]
    yield

    odd_row = jnp.bitwise_and(lax.broadcasted_iota(jnp.int32, (c, 1), 0), 1) == 1
    k_prev_units = [pltpu.roll(k, 1, axis=0) for k in k_units]
    fine = []
    for h in range(n_heads):
        u = h // heads_per_unit
        s_diag = jnp.sum(qf_heads[h] * k_units[u], axis=-1, keepdims=True)
        s_pair = jnp.sum(qf_heads[h] * decay_units[u] * k_prev_units[u], axis=-1, keepdims=True)
        s_pair = jnp.where(odd_row, s_pair, 0.0)
        fine.append(s_diag * v_heads[h] + s_pair * pltpu.roll(v_heads[h], 1, axis=0))
    yield

    scores = [None] * n_heads
    for lvl in range(len(LEVELS)):
        for u, ul in enumerate(unit_lanes):
            e = e_ref[(2 + lvl) * c:(3 + lvl) * c, ul]
            unit_heads = range(u * heads_per_unit, (u + 1) * heads_per_unit)
            q_rows = jnp.concatenate([q16_heads[h] * e for h in unit_heads], axis=0)
            prod = _dot_nt(q_rows, k16_units[u] * e)
            for n, h in enumerate(unit_heads):
                term = masks_ref[lvl] * prod[n * c:(n + 1) * c, :]
                scores[h] = term if scores[h] is None else scores[h] + term
        yield

    states = [state_ref[h] for h in range(n_heads)]
    for h in range(n_heads):
        e_in = e_ref[0:c, unit_lanes[h // heads_per_unit]]
        o_ref[:, h * LANES:(h + 1) * LANES] = (
            _dot(scores[h].astype(BF16), v16_heads[h])
            + _dot(q16_heads[h] * e_in, states[h].astype(BF16)) + fine[h])
    yield
    for h in range(n_heads):
        lanes = unit_lanes[h // heads_per_unit]
        e_out = e_ref[c:2 * c, lanes]
        e_last = elast_ref[SUBLANES - 1:SUBLANES, lanes]
        e_col = jnp.transpose(jnp.broadcast_to(e_last, (LANES, LANES)))
        state_ref[h] = states[h] * e_col + _dot_tn(kown_heads[h] * e_out, v16_heads[h])
    yield


def _head_norm_gate(o, norm_w, gate):
    y = o * lax.rsqrt(jnp.mean(o * o, axis=-1, keepdims=True) + NORM_EPS) * norm_w
    return y * _silu(gate)


def _attention_scratch(ts, n_cols, decay_w, n_heads):
    return [pltpu.VMEM((ts, n_cols), F32),
            pltpu.VMEM((N_DECAY_BLOCKS * CHUNK, decay_w), F32),
            pltpu.VMEM((2, N_DECAY_BLOCKS * CHUNK, decay_w), BF16),
            pltpu.VMEM((2, SUBLANES, decay_w), F32),
            pltpu.VMEM((2, CHUNK, decay_w), F32),
            pltpu.VMEM((2, CHUNK, n_heads * LANES), F32),
            pltpu.VMEM((n_heads, LANES, LANES), F32)]


def _hgrn2_kernel(layer, z_ref, w_ref, lbl_ref, nw_ref, dmat_ref, masks_ref, y_ref,
                  proj_ref, d_ref, e_ref, elast_ref, dec_ref, o_ref, state_ref, key_ref):
    @pl.when(pl.program_id(1) == 0)
    def _():
        state_ref[...] = jnp.zeros_like(state_ref)

    proj_ref[...] = _dot_nt(z_ref[0], w_ref[...])

    logits = [lbl_ref[i:i + 1, :] for i in range(DEPTH)]
    mx = functools.reduce(jnp.maximum, logits)
    ex = [jnp.exp(t - mx) for t in logits]
    tot = functools.reduce(lambda a, b: a + b, ex)
    lb = functools.reduce(lambda a, b: a + b, ex[:layer + 1]) / tot - ex[0] / tot
    nw = nw_ref[...]
    w = BRANCH_W
    ts = proj_ref.shape[0]
    units = [slice(h * LANES, (h + 1) * LANES) for h in range(HG_HEADS)]

    def decay_pieces(ci, slot):
        rows = _chunk_rows(ci)
        zf = proj_ref[rows, w:2 * w]
        en = jnp.exp(-jnp.abs(zf))
        r = 1.0 / (1.0 + en)
        er = en * r
        pos = zf >= 0.0
        sig = jnp.where(pos, r, er)
        nsig = jnp.where(pos, er, r)
        key_ref[slot] = (1.0 - lb) * nsig
        forget = lb + (1.0 - lb) * sig
        dec_ref[slot] = forget
        log_f = jnp.log(forget)
        yield
        yield from _decay_pieces(log_f, dmat_ref, d_ref, e_ref.at[slot], elast_ref.at[slot])

    def score_pieces(ci, slot):
        rows = _chunk_rows(ci)
        q = proj_ref[rows, 0:w] * (HG_DK ** -0.5)
        key = key_ref[slot]
        forget = dec_ref[slot]
        vin = proj_ref[rows, 2 * w:3 * w]
        yield from _score_pieces([q[:, u] for u in units], [key[:, u] for u in units],
                                 [vin[:, u] for u in units], [forget[:, u] for u in units], 1,
                                 masks_ref, e_ref.at[slot], elast_ref.at[slot], state_ref,
                                 o_ref.at[slot])

    def tail_pieces(ci, slot):
        rows = _chunk_rows(ci)
        for h, u in enumerate(units):
            gate = proj_ref[rows, 3 * w + h * LANES:3 * w + (h + 1) * LANES]
            y_ref[0, rows, u] = _head_norm_gate(o_ref[slot, :, u], nw, gate).astype(BF16)
            yield

    _pipelined_chunks(ts // CHUNK, decay_pieces, score_pieces, tail_pieces)


def _hgrn2(z, w_all, lb_logits, norm_w, dmat, masks, layer, ts):
    bsz, seqlen, d = z.shape
    n_cols = _w_cols("hgrn2")
    return pl.pallas_call(
        functools.partial(_hgrn2_kernel, layer),
        grid=(bsz, seqlen // ts),
        in_specs=[_row_spec(ts, d), _w_spec("hgrn2", layer, d), _full_spec(lb_logits.shape),
                  _full_spec((1, LANES)), _full_spec(dmat.shape), _full_spec(masks.shape)],
        out_specs=_row_spec(ts, BRANCH_W),
        out_shape=jax.ShapeDtypeStruct((bsz, seqlen, BRANCH_W), BF16),
        scratch_shapes=_attention_scratch(ts, n_cols, BRANCH_W, HG_HEADS)
                       + [pltpu.VMEM((2, CHUNK, BRANCH_W), F32)],
        compiler_params=_params(2),
        name="hgrn2_mixer",
    )(z, w_all, lb_logits, norm_w.reshape(1, LANES), dmat, masks)


def _gla_kernel(z_ref, w_ref, gkw_ref, gkb_ref, nw_ref, dmat_ref, masks_ref, y_ref,
                proj_ref, d_ref, e_ref, elast_ref, dec_ref, o_ref, state_ref):
    @pl.when(pl.program_id(1) == 0)
    def _():
        state_ref[...] = jnp.zeros_like(state_ref)

    proj_ref[...] = _dot_nt(z_ref[0], w_ref[0:proj_ref.shape[1], :])
    nw = nw_ref[...]
    gkb = gkb_ref[...]
    qk_w = GLA_HEADS * GLA_DK
    w = BRANCH_W
    ts = proj_ref.shape[0]
    units = [slice(u * LANES, (u + 1) * LANES) for u in range(qk_w // LANES)]
    heads = [slice(h * LANES, (h + 1) * LANES) for h in range(GLA_HEADS)]
    gate_off = 2 * qk_w + w

    def decay_pieces(ci, slot):
        rows = _chunk_rows(ci)
        lowrank = proj_ref[rows, gate_off + w:gate_off + w + LANES]
        gk = _dot(lowrank.astype(BF16), gkw_ref[...]) + gkb
        log_alpha = (jnp.minimum(gk, 0.0) - jnp.log(1.0 + jnp.exp(-jnp.abs(gk)))) * (1.0 / GLA_TAU)
        dec_ref[slot] = jnp.exp(log_alpha)
        yield
        yield from _decay_pieces(log_alpha, dmat_ref, d_ref, e_ref.at[slot], elast_ref.at[slot])

    def score_pieces(ci, slot):
        rows = _chunk_rows(ci)
        q = proj_ref[rows, 0:qk_w] * (GLA_DK ** -0.5)
        k = proj_ref[rows, qk_w:2 * qk_w]
        v = proj_ref[rows, 2 * qk_w:2 * qk_w + w]
        alpha = dec_ref[slot]
        yield from _score_pieces([q[:, u] for u in units], [k[:, u] for u in units],
                                 [v[:, hs] for hs in heads], [alpha[:, u] for u in units],
                                 LANES // GLA_DK, masks_ref, e_ref.at[slot], elast_ref.at[slot],
                                 state_ref, o_ref.at[slot])

    def tail_pieces(ci, slot):
        rows = _chunk_rows(ci)
        for h, hs in enumerate(heads):
            gate = proj_ref[rows, gate_off + h * LANES:gate_off + (h + 1) * LANES]
            y_ref[0, rows, hs] = _head_norm_gate(o_ref[slot, :, hs], nw, gate).astype(BF16)
            yield

    _pipelined_chunks(ts // CHUNK, decay_pieces, score_pieces, tail_pieces)


def _gla(z, w_all, gk_w, gk_b, norm_w, dmat, masks, layer, ts):
    bsz, seqlen, d = z.shape
    n_cols = _w_cols("gla")
    qk_w = GLA_HEADS * GLA_DK
    return pl.pallas_call(
        _gla_kernel,
        grid=(bsz, seqlen // ts),
        in_specs=[_row_spec(ts, d), _w_spec("gla", layer, d), _full_spec((LANES, qk_w)),
                  _full_spec((1, qk_w)), _full_spec((1, LANES)), _full_spec(dmat.shape),
                  _full_spec(masks.shape)],
        out_specs=_row_spec(ts, BRANCH_W),
        out_shape=jax.ShapeDtypeStruct((bsz, seqlen, BRANCH_W), BF16),
        scratch_shapes=_attention_scratch(ts, n_cols, qk_w, GLA_HEADS),
        compiler_params=_params(2),
        name="gla_mixer",
    )(z, w_all, gk_w, gk_b.reshape(1, qk_w), norm_w.reshape(1, LANES), dmat, masks)


def _gelu(x):
    return 0.5 * x * (1.0 + lax.erf(x * (2.0 ** -0.5)))


def _sg_kernel(z_ref, w_ref, lnw_ref, lnb_ref, ws_ref, bias_ref, y_ref):
    w = BRANCH_W
    proj = _dot_nt(z_ref[0], w_ref[0:3 * w, :])
    ts = proj.shape[0]
    u = _gelu(proj[:, 0:w])
    vf = _gelu(proj[:, w:2 * w])
    gate = proj[:, 2 * w:3 * w]
    mu = jnp.mean(vf, axis=-1, keepdims=True)
    dv = vf - mu
    var = jnp.mean(dv * dv, axis=-1, keepdims=True)
    vn = (dv * lax.rsqrt(var + NORM_EPS) * lnw_ref[...] + lnb_ref[...]).astype(BF16)
    ri = lax.broadcasted_iota(jnp.int32, (SG_LEN, SG_LEN), 0)
    cj = lax.broadcasted_iota(jnp.int32, (SG_LEN, SG_LEN), 1)
    allowed = (cj < CHUNK) | (ri >= CHUNK)
    wm = [jnp.where(allowed, ws_ref[g], 0.0).astype(BF16) for g in range(SG_GROUPS)]
    bias = bias_ref[...]
    ug = u * _silu(gate)
    for blk in range(ts // SG_LEN):
        rows = slice(blk * SG_LEN, (blk + 1) * SG_LEN)
        for g in range(SG_GROUPS):
            cols = slice(g * LANES, (g + 1) * LANES)
            mixed = _dot(wm[g], vn[rows, cols]) + bias[:, cols]
            y_ref[0, rows, cols] = (ug[rows, cols] * mixed).astype(BF16)


def _spatial_gating(z, w_all, ln_w, ln_b, w_s, bias, layer, ts):
    bsz, seqlen, d = z.shape
    assert _w_cols("sg") == 3 * BRANCH_W
    return pl.pallas_call(
        _sg_kernel,
        grid=(bsz, seqlen // ts),
        in_specs=[_row_spec(ts, d), _w_spec("sg", layer, d), _full_spec((1, BRANCH_W)),
                  _full_spec((1, BRANCH_W)), _full_spec(w_s.shape), _full_spec(bias.shape)],
        out_specs=_row_spec(ts, BRANCH_W),
        out_shape=jax.ShapeDtypeStruct((bsz, seqlen, BRANCH_W), BF16),
        compiler_params=_params(2),
        name="spatial_gating_mixer",
    )(z, w_all, ln_w.reshape(1, BRANCH_W), ln_b.reshape(1, BRANCH_W), w_s, bias)


def _expm1_nonpos(x):
    p = 1.0 / 720
    for coeff in (1.0 / 120, 1.0 / 24, 1.0 / 6, 0.5, 1.0):
        p = p * x + coeff
    return jnp.where(x > -0.125, p * x, jnp.exp(x) - 1.0)


def _lru_kernel(z_ref, w_ref, cw_ref, cb_ref, wax_ref, ba_ref, bx_ref, lam_ref, y_ref,
                xtb_ref, hs_ref, hprev_ref):
    bsz, tt, d = z_ref.shape
    w = BRANCH_W
    n_rows = bsz * tt
    n_slabs = w // LANES
    tail = (CONV_WIDTH - 1) * bsz

    @pl.when(pl.program_id(0) == 0)
    def _():
        xtb_ref[:, 0:tail, :] = jnp.zeros((n_slabs, tail, LANES), F32)
        hprev_ref[...] = jnp.zeros_like(hprev_ref)

    proj = _dot_nt(z_ref[...].reshape(n_rows, d), w_ref[...])
    gate = proj[:, w:2 * w]
    for b in range(bsz):
        rows = slice(b * tt, (b + 1) * tt)
        for j in range(n_slabs):
            xtb_ref[j, pl.ds(tail + b, tt, stride=bsz), :] = proj[rows, j * LANES:(j + 1) * LANES]

    def delayed(back):
        start = tail - back * bsz
        return jnp.concatenate(
            [xtb_ref[j, start:start + n_rows, :] for j in range(n_slabs)], axis=1)

    xc = cb_ref[...]
    for tap in range(CONV_WIDTH):
        xc = xc + delayed(CONV_WIDTH - 1 - tap) * cw_ref[tap:tap + 1, :]
    for j in range(n_slabs):
        xtb_ref[j, 0:tail, :] = xtb_ref[j, n_rows:n_rows + tail, :]

    xc16 = xc.astype(BF16)
    ra, ix = [], []
    for h in range(LRU_HEADS):
        cols = slice(h * LRU_HD, (h + 1) * LRU_HD)
        both = _dot(xc16[:, cols], wax_ref[h])
        ra.append(both[:, 0:LRU_HD])
        ix.append(both[:, LRU_HD:2 * LRU_HD])
    ig = _sigmoid(jnp.concatenate(ix, axis=1) + bx_ref[...])
    nlam = -lam_ref[...]
    softplus = jnp.maximum(nlam, 0.0) + jnp.log(1.0 + jnp.exp(-jnp.abs(nlam)))
    half_scale = (-0.5 * RG_C) * softplus
    log_a = half_scale + half_scale * jnp.tanh(0.5 * (jnp.concatenate(ra, axis=1) + ba_ref[...]))
    a = jnp.exp(log_a)
    var = jnp.maximum(-_expm1_nonpos(2.0 * log_a), SQRT_EPS)
    bt = (var * lax.rsqrt(var)) * (ig * xc)

    hcur = [hprev_ref[j] for j in range(n_slabs)]
    for t in range(tt):
        step = slice(t * bsz, (t + 1) * bsz)
        for j in range(n_slabs):
            cols = slice(j * LANES, (j + 1) * LANES)
            hcur[j] = a[step, cols] * hcur[j] + bt[step, cols]
            hs_ref[j, step, :] = hcur[j]
    for j in range(n_slabs):
        hprev_ref[j] = hcur[j]

    hseq = jnp.concatenate(
        [jnp.concatenate([hs_ref[j, pl.ds(b, tt, stride=bsz), :] for j in range(n_slabs)], axis=1)
         for b in range(bsz)], axis=0)
    y_ref[...] = (hseq * _silu(gate)).astype(BF16).reshape(bsz, tt, w)


def _rglru(z, w_all, conv_w, conv_b, wax, b_a, b_x, lam, layer, tt):
    bsz, seqlen, d = z.shape
    w = BRANCH_W
    assert _w_cols("lru") == 2 * w
    assert bsz == SUBLANES, "the RG-LRU scan puts the batch on the vreg sublanes"
    n_slabs = w // LANES
    return pl.pallas_call(
        _lru_kernel,
        grid=(seqlen // tt,),
        in_specs=[pl.BlockSpec((bsz, tt, d), lambda i: (0, i, 0)), _w_spec("lru", layer, d),
                  _full_spec((CONV_WIDTH, w)), _full_spec((1, w)), _full_spec(wax.shape),
                  _full_spec((1, w)), _full_spec((1, w)), _full_spec((1, w))],
        out_specs=pl.BlockSpec((bsz, tt, w), lambda i: (0, i, 0)),
        out_shape=jax.ShapeDtypeStruct((bsz, seqlen, w), BF16),
        scratch_shapes=[pltpu.VMEM((n_slabs, bsz * (tt + CONV_WIDTH - 1), LANES), F32),
                        pltpu.VMEM((n_slabs, bsz * tt, LANES), F32),
                        pltpu.VMEM((n_slabs, bsz, LANES), F32)],
        compiler_params=_params(1),
        name="rglru_mixer",
    )(z, w_all, conv_w, conv_b.reshape(1, w), wax, b_a.reshape(1, w), b_x.reshape(1, w),
      lam.reshape(1, w))


MERGE_COLS = 256


def _merge_kernel(final, z_ref, h_ref, ya_ref, yb_ref, yc_ref, yd_ref, wmg_ref, wbr_ref,
                  wout_ref, nw_ref, *rest):
    if final:
        out_ref, merged_ref = rest
    else:
        hn_ref, zn_ref, merged_ref = rest
    zt = z_ref[0]
    ys = [ya_ref[0], yb_ref[0], yc_ref[0], yd_ref[0]]
    for n in range(D_MODEL // MERGE_COLS):
        cols = slice(n * MERGE_COLS, (n + 1) * MERGE_COLS)
        merged = None
        for b in range(N_BRANCH):
            gcols = slice(b * D_MODEL + n * MERGE_COLS, b * D_MODEL + (n + 1) * MERGE_COLS)
            gate = _sigmoid(_dot_nt(zt, wmg_ref[gcols, :]))
            term = gate * _dot(ys[b], wbr_ref[b, :, cols])
            merged = term if merged is None else merged + term
        merged_ref[:, cols] = merged.astype(BF16)
    out = h_ref[0] + _dot(merged_ref[...], wout_ref[...])
    normed = _rms_norm_rows(out, nw_ref[...])
    if final:
        out_ref[0] = normed
    else:
        hn_ref[0] = out
        zn_ref[0] = normed.astype(BF16)


def _merge(z, h, ys, w_all, w_br, w_out, next_norm_w, layer, final, ts):
    bsz, seqlen, d = h.shape
    assert _w_cols("merge") == N_BRANCH * d
    f32_out = jax.ShapeDtypeStruct((bsz, seqlen, d), F32)
    if final:
        out_shape, out_specs = f32_out, _row_spec(ts, d)
    else:
        out_shape = (f32_out, jax.ShapeDtypeStruct((bsz, seqlen, d), BF16))
        out_specs = (_row_spec(ts, d), _row_spec(ts, d))
    return pl.pallas_call(
        functools.partial(_merge_kernel, final),
        grid=(bsz, seqlen // ts),
        in_specs=[_row_spec(ts, d), _row_spec(ts, d)] + [_row_spec(ts, BRANCH_W)] * N_BRANCH
                 + [_w_spec("merge", layer, d),
                    pl.BlockSpec((None,) + w_br.shape[1:], lambda *_: (layer, 0, 0, 0),
                                 pipeline_mode=pl.Buffered(1)),
                    pl.BlockSpec((None,) + w_out.shape[1:], lambda *_: (layer, 0, 0),
                                 pipeline_mode=pl.Buffered(1)),
                    _full_spec((1, d))],
        out_specs=out_specs,
        out_shape=out_shape,
        scratch_shapes=[pltpu.VMEM((ts, d), BF16)],
        compiler_params=_params(2),
        name="merge_final" if final else "merge",
    )(z, h, *ys, w_all, w_br, w_out, next_norm_w.reshape(1, d))


TS_NORM = 1024
TS_ATTN = 2048
TS_SG = 1024
TT_LRU = 128
TS_MERGE = 512


def kernel(x, norm_w, w_in, hg_lb_logits, hg_norm_w, gla_gk_w, gla_gk_b, gla_norm_w, sg_ln_w, sg_ln_b, sg_w, sg_b, lru_conv_w, lru_conv_b, lru_w_a, lru_b_a, lru_w_x, lru_b_x, lru_lambda, w_branch, w_out, final_norm_w):
    dmat = jnp.asarray(_decay_matrix(), BF16)
    masks = jnp.asarray(_level_masks(), F32)
    gkw_pad = jnp.zeros((LANES - GLA_RANK, GLA_HEADS * GLA_DK), BF16)
    w_all = _pack_in_proj(w_in)
    w_br = w_branch.astype(BF16)
    w_o = w_out.astype(BF16)

    z = _first_norm(x, norm_w[0], TS_NORM)
    h = x
    for l in range(DEPTH):
        gkw = jnp.concatenate([gla_gk_w[l].astype(BF16), gkw_pad], axis=0)
        wax = jnp.concatenate([lru_w_a[l], lru_w_x[l]], axis=-1).astype(BF16)
        sg_bias = jnp.repeat(sg_b[l].T, LANES, axis=1)

        y_a = _hgrn2(z, w_all, hg_lb_logits, hg_norm_w[l], dmat, masks, l, TS_ATTN)
        y_b = _gla(z, w_all, gkw, gla_gk_b[l], gla_norm_w[l], dmat, masks, l, TS_ATTN)
        y_c = _spatial_gating(z, w_all, sg_ln_w[l], sg_ln_b[l], sg_w[l], sg_bias, l, TS_SG)
        y_d = _rglru(z, w_all, lru_conv_w[l], lru_conv_b[l], wax, lru_b_a[l], lru_b_x[l],
                     lru_lambda[l], l, TT_LRU)
        ys = (y_a, y_b, y_c, y_d)
        if l + 1 < DEPTH:
            h, z = _merge(z, h, ys, w_all, w_br, w_o, norm_w[l + 1], l, False, TS_MERGE)
        else:
            return _merge(z, h, ys, w_all, w_br, w_o, final_norm_w, l, True, TS_MERGE)
```

```python
import functools

import numpy as np
import jax
import jax.numpy as jnp
from jax import lax
from jax.experimental import pallas as pl
from jax.experimental.pallas import tpu as pltpu

F32 = jnp.float32
BF16 = jnp.bfloat16

D_MODEL = 1024
DEPTH = 2
CHUNK = 64
BRANCH_W = 512
N_BRANCH = 4
NORM_EPS = 1e-6
SQRT_EPS = 1e-12
HG_HEADS = 4
HG_DK = 128
GLA_HEADS = 4
GLA_DK = 64
GLA_RANK = 16
GLA_TAU = 16.0
SG_GROUPS = 4
SG_LEN = 128
LRU_HEADS = 4
LRU_HD = 128
CONV_WIDTH = 4
RG_C = 8.0

LANES = 128
SUBLANES = 8
VMEM_LIMIT_BYTES = 56 * 1024 * 1024
LOG2E = 1.4426950408889634

OFF_A = 0
OFF_B = 2048
OFF_LR = 3584
OFF_C = 3600
OFF_D = 5136
OFF_MG = 6160
N_IN = 10256

LEVELS = (32, 16, 8, 4, 2)
N_DECAY_BLOCKS = 2 + len(LEVELS)
SPLIT_TERMS = 3


def _decay_matrix(n_matmul_levels):
    c = CHUNK
    i = np.arange(c)[:, None]
    j = np.arange(c)[None, :]
    blocks = [(j <= i)]
    for s in LEVELS[len(LEVELS) - n_matmul_levels:]:
        ref = (i // (2 * s)) * (2 * s) + s - 1
        in_b = i > ref
        blocks.append(np.where(in_b, (j > ref) & (j <= i), (j > i) & (j <= ref)))
    m = np.concatenate(blocks, axis=0).astype(np.float32)
    return np.concatenate([m] * SPLIT_TERMS, axis=1)


def _level_masks():
    c = CHUNK
    i = np.arange(c)[:, None]
    j = np.arange(c)[None, :]
    masks = []
    for s in LEVELS:
        same = (i // (2 * s)) == (j // (2 * s))
        masks.append(same & ((i // s) % 2 == 1) & ((j // s) % 2 == 0))
    return np.stack(masks).astype(np.float32)


def _sigmoid(x):
    return 0.5 + 0.5 * jnp.tanh(0.5 * x)


def _silu(x):
    half = 0.5 * x
    return half + half * jnp.tanh(half)


def _dot(a, b):
    return jnp.dot(a, b, preferred_element_type=F32)


def _dot_nt(a, b):
    return lax.dot_general(a, b, (((1,), (1,)), ((), ())), preferred_element_type=F32)


def _dot_tn(a, b):
    return lax.dot_general(a, b, (((0,), (0,)), ((), ())), preferred_element_type=F32)


def _split_bf16(x):
    terms = []
    rest = x
    for _ in range(SPLIT_TERMS):
        t = rest.astype(BF16)
        terms.append(t)
        rest = rest - t.astype(F32)
    return jnp.concatenate(terms, axis=0)


def _rms_norm_rows(x, w):
    return x * lax.rsqrt(jnp.mean(x * x, axis=-1, keepdims=True) + NORM_EPS) * w


def _params(n_grid_dims):
    return pltpu.CompilerParams(
        dimension_semantics=("arbitrary",) * n_grid_dims,
        vmem_limit_bytes=VMEM_LIMIT_BYTES,
    )


def _full_spec(shape):
    return pl.BlockSpec(shape, lambda *_: (0,) * len(shape))


def _row_spec(ts, width):
    return pl.BlockSpec((1, ts, width), lambda b, s: (b, s, 0))


_W_SEGMENTS = (("hgrn2", 2048, OFF_B - OFF_A), ("gla", 2048, OFF_LR - OFF_B + LANES),
               ("merge", 4096, N_IN - OFF_MG), ("sg", 2048, OFF_D - OFF_C),
               ("lru", 1024, OFF_MG - OFF_D))
_W_SOURCE = {"hgrn2": (OFF_A, OFF_B), "gla": (OFF_B, OFF_C), "merge": (OFF_MG, N_IN),
             "sg": (OFF_C, OFF_D), "lru": (OFF_D, OFF_MG)}


def _pack_in_proj(w_in):
    w16 = jnp.swapaxes(w_in, 1, 2).astype(BF16)
    parts = []
    for name, height, _ in _W_SEGMENTS:
        lo, hi = _W_SOURCE[name]
        parts.append(w16[:, lo:hi, :])
        if height > hi - lo:
            parts.append(jnp.zeros((w16.shape[0], height - (hi - lo), w16.shape[2]), BF16))
    return jnp.concatenate(parts, axis=1)


def _w_spec(name, layer, d):
    offset = 0
    for seg, height, _ in _W_SEGMENTS:
        if seg == name:
            assert offset % height == 0
            return pl.BlockSpec((None, height, d), lambda *_: (layer, offset // height, 0),
                                pipeline_mode=pl.Buffered(1))
        offset += height
    raise KeyError(name)


def _w_cols(name):
    return {seg: used for seg, _, used in _W_SEGMENTS}[name]


def _norm_kernel(x_ref, w_ref, z_ref):
    z_ref[0] = _rms_norm_rows(x_ref[0], w_ref[...]).astype(BF16)


def _first_norm(x, w, ts):
    bsz, seqlen, d = x.shape
    return pl.pallas_call(
        _norm_kernel,
        grid=(bsz, seqlen // ts),
        in_specs=[_row_spec(ts, d), _full_spec((1, d))],
        out_specs=_row_spec(ts, d),
        out_shape=jax.ShapeDtypeStruct((bsz, seqlen, d), BF16),
        compiler_params=_params(2),
        name="first_norm",
    )(x, w.reshape(1, d))


CHUNKS_PER_ITERATION = 2


def _interleave(*pieces):
    live = list(pieces)
    while live:
        for p in list(live):
            try:
                next(p)
            except StopIteration:
                live.remove(p)


def _pipelined_chunks(n_chunks, decay_pieces, score_pieces, tail_pieces):
    assert n_chunks % 2 == 0 and n_chunks >= 4 and CHUNKS_PER_ITERATION % 2 == 0

    def step(c, slot):
        _interleave(score_pieces(c, slot), decay_pieces(c + 1, 1 - slot),
                    tail_pieces(c - 1, 1 - slot))

    _interleave(decay_pieces(0, 0))
    _interleave(decay_pieces(1, 1), score_pieces(0, 0))
    n_loop = (n_chunks - 2) // CHUNKS_PER_ITERATION

    def body(i, carry):
        first = CHUNKS_PER_ITERATION * i + 1
        for k in range(CHUNKS_PER_ITERATION):
            step(first + k, (1 + k) % 2)
        return carry

    lax.fori_loop(0, n_loop, body, 0)
    for c in range(1 + n_loop * CHUNKS_PER_ITERATION, n_chunks - 1):
        step(c, c % 2)
    _interleave(score_pieces(n_chunks - 1, 1), tail_pieces(n_chunks - 2, 0))
    _interleave(tail_pieces(n_chunks - 1, 1))


def _chunk_rows(ci):
    start = ci * CHUNK
    if not isinstance(ci, int):
        start = pl.multiple_of(start, CHUNK)
    return pl.ds(start, CHUNK)


def _decay_pieces(log_decay, dmat_ref, d_ref, e_ref, elast_ref):
    c = CHUNK
    width = d_ref.shape[1]
    n_matmul_levels = dmat_ref.shape[0] // c - 1
    vpu_levels = LEVELS[:len(LEVELS) - n_matmul_levels]
    matmul_levels = LEVELS[len(LEVELS) - n_matmul_levels:]
    assert all(2 * s % SUBLANES == 0 for s in vpu_levels)
    d_ref[...] = _dot(dmat_ref[...], _split_bf16(log_decay * LOG2E))
    yield

    def put(blk, exponent):
        ex = jnp.exp2(exponent)
        e_ref[blk * c:(blk + 1) * c, :] = ex.astype(BF16)
        return ex

    elast_ref[...] = put(0, d_ref[0:c, :])[c - SUBLANES:c, :]
    yield
    b = d_ref[0:c, :]
    put(1, b[c - 1:c, :] - b)
    yield
    for s in vpu_levels:
        b = d_ref[0:c, :]
        ref = jnp.concatenate(
            [jnp.broadcast_to(b[g + s - 1:g + s, :], (2 * s, width)) for g in range(0, c, 2 * s)],
            axis=0)
        put(2 + LEVELS.index(s), -jnp.abs(b - ref))
        yield
    for n, s in enumerate(matmul_levels):
        put(2 + LEVELS.index(s), d_ref[(1 + n) * c:(2 + n) * c, :])
        yield


def _score_pieces(q_units, k_units, v_heads, decay_units, heads_per_unit, masks_ref, e_ref,
                  elast_ref, state_ref, o_ref):
    c = CHUNK
    n_heads = len(v_heads)
    lane = lax.broadcasted_iota(jnp.int32, (c, LANES), 1)
    head_w = LANES // heads_per_unit
    unit_lanes = [slice(u * LANES, (u + 1) * LANES) for u in range(len(q_units))]
    k16_units = [k.astype(BF16) for k in k_units]
    qf_heads, q16_heads, kown_heads = [], [], []
    for h in range(n_heads):
        u = h // heads_per_unit
        qf = q_units[u]
        k16 = k16_units[u]
        if heads_per_unit > 1:
            sub = h % heads_per_unit
            in_head = (lane >= sub * head_w) & (lane < (sub + 1) * head_w)
            qf = jnp.where(in_head, qf, 0.0)
            kown_heads.append(jnp.where(in_head, k16, jnp.zeros_like(k16)))
        else:
            kown_heads.append(k16)
        qf_heads.append(qf)
        q16_heads.append(qf.astype(BF16))
    v16_heads = [v.astype(BF16) for v in v_heads]
    yield

    odd_row = jnp.bitwise_and(lax.broadcasted_iota(jnp.int32, (c, 1), 0), 1) == 1
    k_prev_units = [pltpu.roll(k, 1, axis=0) for k in k_units]
    fine = []
    for h in range(n_heads):
        u = h // heads_per_unit
        s_diag = jnp.sum(qf_heads[h] * k_units[u], axis=-1, keepdims=True)
        s_pair = jnp.sum(qf_heads[h] * decay_units[u] * k_prev_units[u], axis=-1, keepdims=True)
        s_pair = jnp.where(odd_row, s_pair, 0.0)
        fine.append(s_diag * v_heads[h] + s_pair * pltpu.roll(v_heads[h], 1, axis=0))
    yield

    scores = [None] * n_heads
    for lvl in range(len(LEVELS)):
        for u, ul in enumerate(unit_lanes):
            e = e_ref[(2 + lvl) * c:(3 + lvl) * c, ul]
            unit_heads = range(u * heads_per_unit, (u + 1) * heads_per_unit)
            q_rows = jnp.concatenate([q16_heads[h] * e for h in unit_heads], axis=0)
            prod = _dot_nt(q_rows, k16_units[u] * e)
            for n, h in enumerate(unit_heads):
                term = masks_ref[lvl] * prod[n * c:(n + 1) * c, :]
                scores[h] = term if scores[h] is None else scores[h] + term
        yield

    states = [state_ref[h] for h in range(n_heads)]
    for h in range(n_heads):
        e_in = e_ref[0:c, unit_lanes[h // heads_per_unit]]
        o_ref[:, h * LANES:(h + 1) * LANES] = (
            _dot(scores[h].astype(BF16), v16_heads[h])
            + _dot(q16_heads[h] * e_in, states[h].astype(BF16)) + fine[h])
    yield
    for h in range(n_heads):
        lanes = unit_lanes[h // heads_per_unit]
        e_out = e_ref[c:2 * c, lanes]
        e_last = elast_ref[SUBLANES - 1:SUBLANES, lanes]
        e_col = jnp.transpose(jnp.broadcast_to(e_last, (LANES, LANES)))
        state_ref[h] = states[h] * e_col + _dot_tn(kown_heads[h] * e_out, v16_heads[h])
    yield


def _head_norm_gate(o, norm_w, gate):
    y = o * lax.rsqrt(jnp.mean(o * o, axis=-1, keepdims=True) + NORM_EPS) * norm_w
    return y * _silu(gate)


def _attention_scratch(ts, n_cols, decay_w, n_heads, dmat_rows):
    return [pltpu.VMEM((ts, n_cols), F32),
            pltpu.VMEM((dmat_rows, decay_w), F32),
            pltpu.VMEM((2, N_DECAY_BLOCKS * CHUNK, decay_w), BF16),
            pltpu.VMEM((2, SUBLANES, decay_w), F32),
            pltpu.VMEM((2, CHUNK, decay_w), F32),
            pltpu.VMEM((2, CHUNK, n_heads * LANES), F32),
            pltpu.VMEM((n_heads, LANES, LANES), F32)]


def _hgrn2_kernel(layer, z_ref, w_ref, lbl_ref, nw_ref, dmat_ref, masks_ref, y_ref,
                  proj_ref, d_ref, e_ref, elast_ref, dec_ref, o_ref, state_ref, key_ref):
    @pl.when(pl.program_id(1) == 0)
    def _():
        state_ref[...] = jnp.zeros_like(state_ref)

    proj_ref[...] = _dot_nt(z_ref[0], w_ref[...])

    logits = [lbl_ref[i:i + 1, :] for i in range(DEPTH)]
    mx = functools.reduce(jnp.maximum, logits)
    ex = [jnp.exp(t - mx) for t in logits]
    tot = functools.reduce(lambda a, b: a + b, ex)
    lb = functools.reduce(lambda a, b: a + b, ex[:layer + 1]) / tot - ex[0] / tot
    nw = nw_ref[...]
    w = BRANCH_W
    ts = proj_ref.shape[0]
    units = [slice(h * LANES, (h + 1) * LANES) for h in range(HG_HEADS)]

    def decay_pieces(ci, slot):
        rows = _chunk_rows(ci)
        zf = proj_ref[rows, w:2 * w]
        en = jnp.exp(-jnp.abs(zf))
        r = 1.0 / (1.0 + en)
        er = en * r
        pos = zf >= 0.0
        sig = jnp.where(pos, r, er)
        nsig = jnp.where(pos, er, r)
        key_ref[slot] = (1.0 - lb) * nsig
        forget = lb + (1.0 - lb) * sig
        dec_ref[slot] = forget
        log_f = jnp.log(forget)
        yield
        yield from _decay_pieces(log_f, dmat_ref, d_ref, e_ref.at[slot], elast_ref.at[slot])

    def score_pieces(ci, slot):
        rows = _chunk_rows(ci)
        q = proj_ref[rows, 0:w] * (HG_DK ** -0.5)
        key = key_ref[slot]
        forget = dec_ref[slot]
        vin = proj_ref[rows, 2 * w:3 * w]
        yield from _score_pieces([q[:, u] for u in units], [key[:, u] for u in units],
                                 [vin[:, u] for u in units], [forget[:, u] for u in units], 1,
                                 masks_ref, e_ref.at[slot], elast_ref.at[slot], state_ref,
                                 o_ref.at[slot])

    def tail_pieces(ci, slot):
        rows = _chunk_rows(ci)
        for h, u in enumerate(units):
            gate = proj_ref[rows, 3 * w + h * LANES:3 * w + (h + 1) * LANES]
            y_ref[0, rows, u] = _head_norm_gate(o_ref[slot, :, u], nw, gate).astype(BF16)
            yield

    _pipelined_chunks(ts // CHUNK, decay_pieces, score_pieces, tail_pieces)


def _hgrn2(z, w_all, lb_logits, norm_w, dmat, masks, layer, ts):
    bsz, seqlen, d = z.shape
    n_cols = _w_cols("hgrn2")
    return pl.pallas_call(
        functools.partial(_hgrn2_kernel, layer),
        grid=(bsz, seqlen // ts),
        in_specs=[_row_spec(ts, d), _w_spec("hgrn2", layer, d), _full_spec(lb_logits.shape),
                  _full_spec((1, LANES)), _full_spec(dmat.shape), _full_spec(masks.shape)],
        out_specs=_row_spec(ts, BRANCH_W),
        out_shape=jax.ShapeDtypeStruct((bsz, seqlen, BRANCH_W), BF16),
        scratch_shapes=_attention_scratch(ts, n_cols, BRANCH_W, HG_HEADS, dmat.shape[0])
                       + [pltpu.VMEM((2, CHUNK, BRANCH_W), F32)],
        compiler_params=_params(2),
        name="hgrn2_mixer",
    )(z, w_all, lb_logits, norm_w.reshape(1, LANES), dmat, masks)


def _gla_kernel(z_ref, w_ref, gkw_ref, gkb_ref, nw_ref, dmat_ref, masks_ref, y_ref,
                proj_ref, d_ref, e_ref, elast_ref, dec_ref, o_ref, state_ref):
    @pl.when(pl.program_id(1) == 0)
    def _():
        state_ref[...] = jnp.zeros_like(state_ref)

    proj_ref[...] = _dot_nt(z_ref[0], w_ref[0:proj_ref.shape[1], :])
    nw = nw_ref[...]
    gkb = gkb_ref[...]
    qk_w = GLA_HEADS * GLA_DK
    w = BRANCH_W
    ts = proj_ref.shape[0]
    units = [slice(u * LANES, (u + 1) * LANES) for u in range(qk_w // LANES)]
    heads = [slice(h * LANES, (h + 1) * LANES) for h in range(GLA_HEADS)]
    gate_off = 2 * qk_w + w

    def decay_pieces(ci, slot):
        rows = _chunk_rows(ci)
        lowrank = proj_ref[rows, gate_off + w:gate_off + w + LANES]
        gk = _dot(lowrank.astype(BF16), gkw_ref[...]) + gkb
        log_alpha = (jnp.minimum(gk, 0.0) - jnp.log(1.0 + jnp.exp(-jnp.abs(gk)))) * (1.0 / GLA_TAU)
        dec_ref[slot] = jnp.exp(log_alpha)
        yield
        yield from _decay_pieces(log_alpha, dmat_ref, d_ref, e_ref.at[slot], elast_ref.at[slot])

    def score_pieces(ci, slot):
        rows = _chunk_rows(ci)
        q = proj_ref[rows, 0:qk_w] * (GLA_DK ** -0.5)
        k = proj_ref[rows, qk_w:2 * qk_w]
        v = proj_ref[rows, 2 * qk_w:2 * qk_w + w]
        alpha = dec_ref[slot]
        yield from _score_pieces([q[:, u] for u in units], [k[:, u] for u in units],
                                 [v[:, hs] for hs in heads], [alpha[:, u] for u in units],
                                 LANES // GLA_DK, masks_ref, e_ref.at[slot], elast_ref.at[slot],
                                 state_ref, o_ref.at[slot])

    def tail_pieces(ci, slot):
        rows = _chunk_rows(ci)
        for h, hs in enumerate(heads):
            gate = proj_ref[rows, gate_off + h * LANES:gate_off + (h + 1) * LANES]
            y_ref[0, rows, hs] = _head_norm_gate(o_ref[slot, :, hs], nw, gate).astype(BF16)
            yield

    _pipelined_chunks(ts // CHUNK, decay_pieces, score_pieces, tail_pieces)


def _gla(z, w_all, gk_w, gk_b, norm_w, dmat, masks, layer, ts):
    bsz, seqlen, d = z.shape
    n_cols = _w_cols("gla")
    qk_w = GLA_HEADS * GLA_DK
    return pl.pallas_call(
        _gla_kernel,
        grid=(bsz, seqlen // ts),
        in_specs=[_row_spec(ts, d), _w_spec("gla", layer, d), _full_spec((LANES, qk_w)),
                  _full_spec((1, qk_w)), _full_spec((1, LANES)), _full_spec(dmat.shape),
                  _full_spec(masks.shape)],
        out_specs=_row_spec(ts, BRANCH_W),
        out_shape=jax.ShapeDtypeStruct((bsz, seqlen, BRANCH_W), BF16),
        scratch_shapes=_attention_scratch(ts, n_cols, qk_w, GLA_HEADS, dmat.shape[0]),
        compiler_params=_params(2),
        name="gla_mixer",
    )(z, w_all, gk_w, gk_b.reshape(1, qk_w), norm_w.reshape(1, LANES), dmat, masks)


def _gelu(x):
    return 0.5 * x * (1.0 + lax.erf(x * (2.0 ** -0.5)))


def _sg_kernel(z_ref, w_ref, lnw_ref, lnb_ref, ws_ref, bias_ref, y_ref):
    w = BRANCH_W
    proj = _dot_nt(z_ref[0], w_ref[0:3 * w, :])
    ts = proj.shape[0]
    u = _gelu(proj[:, 0:w])
    vf = _gelu(proj[:, w:2 * w])
    gate = proj[:, 2 * w:3 * w]
    mu = jnp.mean(vf, axis=-1, keepdims=True)
    dv = vf - mu
    var = jnp.mean(dv * dv, axis=-1, keepdims=True)
    vn = (dv * lax.rsqrt(var + NORM_EPS) * lnw_ref[...] + lnb_ref[...]).astype(BF16)
    ri = lax.broadcasted_iota(jnp.int32, (SG_LEN, SG_LEN), 0)
    cj = lax.broadcasted_iota(jnp.int32, (SG_LEN, SG_LEN), 1)
    allowed = (cj < CHUNK) | (ri >= CHUNK)
    wm = [jnp.where(allowed, ws_ref[g], 0.0).astype(BF16) for g in range(SG_GROUPS)]
    bias = bias_ref[...]
    ug = u * _silu(gate)
    for blk in range(ts // SG_LEN):
        rows = slice(blk * SG_LEN, (blk + 1) * SG_LEN)
        for g in range(SG_GROUPS):
            cols = slice(g * LANES, (g + 1) * LANES)
            mixed = _dot(wm[g], vn[rows, cols]) + bias[:, cols]
            y_ref[0, rows, cols] = (ug[rows, cols] * mixed).astype(BF16)


def _spatial_gating(z, w_all, ln_w, ln_b, w_s, bias, layer, ts):
    bsz, seqlen, d = z.shape
    assert _w_cols("sg") == 3 * BRANCH_W
    return pl.pallas_call(
        _sg_kernel,
        grid=(bsz, seqlen // ts),
        in_specs=[_row_spec(ts, d), _w_spec("sg", layer, d), _full_spec((1, BRANCH_W)),
                  _full_spec((1, BRANCH_W)), _full_spec(w_s.shape), _full_spec(bias.shape)],
        out_specs=_row_spec(ts, BRANCH_W),
        out_shape=jax.ShapeDtypeStruct((bsz, seqlen, BRANCH_W), BF16),
        compiler_params=_params(2),
        name="spatial_gating_mixer",
    )(z, w_all, ln_w.reshape(1, BRANCH_W), ln_b.reshape(1, BRANCH_W), w_s, bias)


def _expm1_nonpos(x):
    p = 1.0 / 720
    for coeff in (1.0 / 120, 1.0 / 24, 1.0 / 6, 0.5, 1.0):
        p = p * x + coeff
    return jnp.where(x > -0.125, p * x, jnp.exp(x) - 1.0)


def _lru_kernel(z_ref, w_ref, cw_ref, cb_ref, wax_ref, ba_ref, bx_ref, lam_ref, y_ref,
                xtb_ref, hs_ref, hprev_ref):
    bsz, tt, d = z_ref.shape
    w = BRANCH_W
    n_rows = bsz * tt
    n_slabs = w // LANES
    tail = (CONV_WIDTH - 1) * bsz

    @pl.when(pl.program_id(0) == 0)
    def _():
        xtb_ref[:, 0:tail, :] = jnp.zeros((n_slabs, tail, LANES), F32)
        hprev_ref[...] = jnp.zeros_like(hprev_ref)

    proj = _dot_nt(z_ref[...].reshape(n_rows, d), w_ref[...])
    gate = proj[:, w:2 * w]
    for b in range(bsz):
        rows = slice(b * tt, (b + 1) * tt)
        for j in range(n_slabs):
            xtb_ref[j, pl.ds(tail + b, tt, stride=bsz), :] = proj[rows, j * LANES:(j + 1) * LANES]

    def delayed(back):
        start = tail - back * bsz
        return jnp.concatenate(
            [xtb_ref[j, start:start + n_rows, :] for j in range(n_slabs)], axis=1)

    xc = cb_ref[...]
    for tap in range(CONV_WIDTH):
        xc = xc + delayed(CONV_WIDTH - 1 - tap) * cw_ref[tap:tap + 1, :]
    for j in range(n_slabs):
        xtb_ref[j, 0:tail, :] = xtb_ref[j, n_rows:n_rows + tail, :]

    xc16 = xc.astype(BF16)
    ra, ix = [], []
    for h in range(LRU_HEADS):
        cols = slice(h * LRU_HD, (h + 1) * LRU_HD)
        both = _dot(xc16[:, cols], wax_ref[h])
        ra.append(both[:, 0:LRU_HD])
        ix.append(both[:, LRU_HD:2 * LRU_HD])
    ig = _sigmoid(jnp.concatenate(ix, axis=1) + bx_ref[...])
    nlam = -lam_ref[...]
    softplus = jnp.maximum(nlam, 0.0) + jnp.log(1.0 + jnp.exp(-jnp.abs(nlam)))
    half_scale = (-0.5 * RG_C) * softplus
    log_a = half_scale + half_scale * jnp.tanh(0.5 * (jnp.concatenate(ra, axis=1) + ba_ref[...]))
    a = jnp.exp(log_a)
    var = jnp.maximum(-_expm1_nonpos(2.0 * log_a), SQRT_EPS)
    bt = (var * lax.rsqrt(var)) * (ig * xc)

    hcur = [hprev_ref[j] for j in range(n_slabs)]
    for t in range(tt):
        step = slice(t * bsz, (t + 1) * bsz)
        for j in range(n_slabs):
            cols = slice(j * LANES, (j + 1) * LANES)
            hcur[j] = a[step, cols] * hcur[j] + bt[step, cols]
            hs_ref[j, step, :] = hcur[j]
    for j in range(n_slabs):
        hprev_ref[j] = hcur[j]

    hseq = jnp.concatenate(
        [jnp.concatenate([hs_ref[j, pl.ds(b, tt, stride=bsz), :] for j in range(n_slabs)], axis=1)
         for b in range(bsz)], axis=0)
    y_ref[...] = (hseq * _silu(gate)).astype(BF16).reshape(bsz, tt, w)


def _rglru(z, w_all, conv_w, conv_b, wax, b_a, b_x, lam, layer, tt):
    bsz, seqlen, d = z.shape
    w = BRANCH_W
    assert _w_cols("lru") == 2 * w
    assert bsz == SUBLANES, "the RG-LRU scan puts the batch on the vreg sublanes"
    n_slabs = w // LANES
    return pl.pallas_call(
        _lru_kernel,
        grid=(seqlen // tt,),
        in_specs=[pl.BlockSpec((bsz, tt, d), lambda i: (0, i, 0)), _w_spec("lru", layer, d),
                  _full_spec((CONV_WIDTH, w)), _full_spec((1, w)), _full_spec(wax.shape),
                  _full_spec((1, w)), _full_spec((1, w)), _full_spec((1, w))],
        out_specs=pl.BlockSpec((bsz, tt, w), lambda i: (0, i, 0)),
        out_shape=jax.ShapeDtypeStruct((bsz, seqlen, w), BF16),
        scratch_shapes=[pltpu.VMEM((n_slabs, bsz * (tt + CONV_WIDTH - 1), LANES), F32),
                        pltpu.VMEM((n_slabs, bsz * tt, LANES), F32),
                        pltpu.VMEM((n_slabs, bsz, LANES), F32)],
        compiler_params=_params(1),
        name="rglru_mixer",
    )(z, w_all, conv_w, conv_b.reshape(1, w), wax, b_a.reshape(1, w), b_x.reshape(1, w),
      lam.reshape(1, w))


MERGE_COLS = 256


def _merge_kernel(final, z_ref, h_ref, ya_ref, yb_ref, yc_ref, yd_ref, wmg_ref, wbr_ref,
                  wout_ref, nw_ref, *rest):
    if final:
        out_ref, merged_ref = rest
    else:
        hn_ref, zn_ref, merged_ref = rest
    zt = z_ref[0]
    ys = [ya_ref[0], yb_ref[0], yc_ref[0], yd_ref[0]]
    for n in range(D_MODEL // MERGE_COLS):
        cols = slice(n * MERGE_COLS, (n + 1) * MERGE_COLS)
        merged = None
        for b in range(N_BRANCH):
            gcols = slice(b * D_MODEL + n * MERGE_COLS, b * D_MODEL + (n + 1) * MERGE_COLS)
            gate = _sigmoid(_dot_nt(zt, wmg_ref[gcols, :]))
            term = gate * _dot(ys[b], wbr_ref[b, :, cols])
            merged = term if merged is None else merged + term
        merged_ref[:, cols] = merged.astype(BF16)
    out = h_ref[0] + _dot(merged_ref[...], wout_ref[...])
    normed = _rms_norm_rows(out, nw_ref[...])
    if final:
        out_ref[0] = normed
    else:
        hn_ref[0] = out
        zn_ref[0] = normed.astype(BF16)


def _merge(z, h, ys, w_all, w_br, w_out, next_norm_w, layer, final, ts):
    bsz, seqlen, d = h.shape
    assert _w_cols("merge") == N_BRANCH * d
    f32_out = jax.ShapeDtypeStruct((bsz, seqlen, d), F32)
    if final:
        out_shape, out_specs = f32_out, _row_spec(ts, d)
    else:
        out_shape = (f32_out, jax.ShapeDtypeStruct((bsz, seqlen, d), BF16))
        out_specs = (_row_spec(ts, d), _row_spec(ts, d))
    return pl.pallas_call(
        functools.partial(_merge_kernel, final),
        grid=(bsz, seqlen // ts),
        in_specs=[_row_spec(ts, d), _row_spec(ts, d)] + [_row_spec(ts, BRANCH_W)] * N_BRANCH
                 + [_w_spec("merge", layer, d),
                    pl.BlockSpec((None,) + w_br.shape[1:], lambda *_: (layer, 0, 0, 0),
                                 pipeline_mode=pl.Buffered(1)),
                    pl.BlockSpec((None,) + w_out.shape[1:], lambda *_: (layer, 0, 0),
                                 pipeline_mode=pl.Buffered(1)),
                    _full_spec((1, d))],
        out_specs=out_specs,
        out_shape=out_shape,
        scratch_shapes=[pltpu.VMEM((ts, d), BF16)],
        compiler_params=_params(2),
        name="merge_final" if final else "merge",
    )(z, h, *ys, w_all, w_br, w_out, next_norm_w.reshape(1, d))


HGRN2_MATMUL_LEVELS = 4
GLA_MATMUL_LEVELS = 2
TS_NORM = 1024
TS_ATTN = 2048
TS_SG = 1024
TT_LRU = 128
TS_MERGE = 512


def kernel(x, norm_w, w_in, hg_lb_logits, hg_norm_w, gla_gk_w, gla_gk_b, gla_norm_w, sg_ln_w, sg_ln_b, sg_w, sg_b, lru_conv_w, lru_conv_b, lru_w_a, lru_b_a, lru_w_x, lru_b_x, lru_lambda, w_branch, w_out, final_norm_w):
    dmat_hgrn2 = jnp.asarray(_decay_matrix(HGRN2_MATMUL_LEVELS), BF16)
    dmat_gla = jnp.asarray(_decay_matrix(GLA_MATMUL_LEVELS), BF16)
    masks = jnp.asarray(_level_masks(), F32)
    gkw_pad = jnp.zeros((LANES - GLA_RANK, GLA_HEADS * GLA_DK), BF16)
    w_all = _pack_in_proj(w_in)
    w_br = w_branch.astype(BF16)
    w_o = w_out.astype(BF16)

    z = _first_norm(x, norm_w[0], TS_NORM)
    h = x
    for l in range(DEPTH):
        gkw = jnp.concatenate([gla_gk_w[l].astype(BF16), gkw_pad], axis=0)
        wax = jnp.concatenate([lru_w_a[l], lru_w_x[l]], axis=-1).astype(BF16)
        sg_bias = jnp.repeat(sg_b[l].T, LANES, axis=1)

        y_a = _hgrn2(z, w_all, hg_lb_logits, hg_norm_w[l], dmat_hgrn2, masks, l, TS_ATTN)
        y_b = _gla(z, w_all, gkw, gla_gk_b[l], gla_norm_w[l], dmat_gla, masks, l, TS_ATTN)
        y_c = _spatial_gating(z, w_all, sg_ln_w[l], sg_ln_b[l], sg_w[l], sg_bias, l, TS_SG)
        y_d = _rglru(z, w_all, lru_conv_w[l], lru_conv_b[l], wax, lru_b_a[l], lru_b_x[l],
                     lru_lambda[l], l, TT_LRU)
        ys = (y_a, y_b, y_c, y_d)
        if l + 1 < DEPTH:
            h, z = _merge(z, h, ys, w_all, w_br, w_o, norm_w[l + 1], l, False, TS_MERGE)
        else:
            return _merge(z, h, ys, w_all, w_br, w_o, final_norm_w, l, True, TS_MERGE)
```

```python
import functools

import numpy as np
import jax
import jax.numpy as jnp
from jax import lax
from jax.experimental import pallas as pl
from jax.experimental.pallas import tpu as pltpu

F32 = jnp.float32
BF16 = jnp.bfloat16

D_MODEL = 1024
DEPTH = 2
CHUNK = 64
BRANCH_W = 512
N_BRANCH = 4
NORM_EPS = 1e-6
SQRT_EPS = 1e-12
HG_HEADS = 4
HG_DK = 128
GLA_HEADS = 4
GLA_DK = 64
GLA_RANK = 16
GLA_TAU = 16.0
SG_GROUPS = 4
SG_LEN = 128
LRU_HEADS = 4
LRU_HD = 128
CONV_WIDTH = 4
RG_C = 8.0

LANES = 128
SUBLANES = 8
VMEM_LIMIT_BYTES = 56 * 1024 * 1024
LOG2E = 1.4426950408889634

OFF_A = 0
OFF_B = 2048
OFF_LR = 3584
OFF_C = 3600
OFF_D = 5136
OFF_MG = 6160
N_IN = 10256

LEVELS = (32, 16, 8, 4, 2)
N_DECAY_BLOCKS = 2 + len(LEVELS)
SPLIT_TERMS = 3


def _decay_matrix(n_matmul_levels):
    c = CHUNK
    i = np.arange(c)[:, None]
    j = np.arange(c)[None, :]
    blocks = [(j <= i)]
    for s in LEVELS[len(LEVELS) - n_matmul_levels:]:
        ref = (i // (2 * s)) * (2 * s) + s - 1
        in_b = i > ref
        blocks.append(np.where(in_b, (j > ref) & (j <= i), (j > i) & (j <= ref)))
    m = np.concatenate(blocks, axis=0).astype(np.float32)
    return np.concatenate([m] * SPLIT_TERMS, axis=1)


def _level_masks():
    c = CHUNK
    i = np.arange(c)[:, None]
    j = np.arange(c)[None, :]
    masks = []
    for s in LEVELS:
        same = (i // (2 * s)) == (j // (2 * s))
        masks.append(same & ((i // s) % 2 == 1) & ((j // s) % 2 == 0))
    return np.stack(masks).astype(np.float32)


def _sigmoid(x):
    return 0.5 + 0.5 * jnp.tanh(0.5 * x)


def _silu(x):
    half = 0.5 * x
    return half + half * jnp.tanh(half)


def _dot(a, b):
    return jnp.dot(a, b, preferred_element_type=F32)


def _dot_nt(a, b):
    return lax.dot_general(a, b, (((1,), (1,)), ((), ())), preferred_element_type=F32)


def _dot_tn(a, b):
    return lax.dot_general(a, b, (((0,), (0,)), ((), ())), preferred_element_type=F32)


def _split_bf16(x):
    terms = []
    rest = x
    for _ in range(SPLIT_TERMS):
        t = rest.astype(BF16)
        terms.append(t)
        rest = rest - t.astype(F32)
    return jnp.concatenate(terms, axis=0)


def _rms_norm_rows(x, w):
    return x * lax.rsqrt(jnp.mean(x * x, axis=-1, keepdims=True) + NORM_EPS) * w


def _params(n_grid_dims):
    return pltpu.CompilerParams(
        dimension_semantics=("arbitrary",) * n_grid_dims,
        vmem_limit_bytes=VMEM_LIMIT_BYTES,
    )


def _full_spec(shape):
    return pl.BlockSpec(shape, lambda *_: (0,) * len(shape))


def _row_spec(ts, width):
    return pl.BlockSpec((1, ts, width), lambda b, s: (b, s, 0))


_W_SEGMENTS = (("hgrn2", 2048, OFF_B - OFF_A), ("gla", 2048, OFF_LR - OFF_B + LANES),
               ("merge", 4096, N_IN - OFF_MG), ("sg", 2048, OFF_D - OFF_C),
               ("lru", 1024, OFF_MG - OFF_D))
_W_SOURCE = {"hgrn2": (OFF_A, OFF_B), "gla": (OFF_B, OFF_C), "merge": (OFF_MG, N_IN),
             "sg": (OFF_C, OFF_D), "lru": (OFF_D, OFF_MG)}


def _pack_in_proj(w_in):
    w16 = jnp.swapaxes(w_in, 1, 2).astype(BF16)
    parts = []
    for name, height, _ in _W_SEGMENTS:
        lo, hi = _W_SOURCE[name]
        parts.append(w16[:, lo:hi, :])
        if height > hi - lo:
            parts.append(jnp.zeros((w16.shape[0], height - (hi - lo), w16.shape[2]), BF16))
    return jnp.concatenate(parts, axis=1)


def _w_spec(name, layer, d):
    offset = 0
    for seg, height, _ in _W_SEGMENTS:
        if seg == name:
            assert offset % height == 0
            return pl.BlockSpec((None, height, d), lambda *_: (layer, offset // height, 0),
                                pipeline_mode=pl.Buffered(1))
        offset += height
    raise KeyError(name)


def _w_cols(name):
    return {seg: used for seg, _, used in _W_SEGMENTS}[name]


def _norm_kernel(x_ref, w_ref, z_ref):
    z_ref[0] = _rms_norm_rows(x_ref[0], w_ref[...]).astype(BF16)


def _first_norm(x, w, ts):
    bsz, seqlen, d = x.shape
    return pl.pallas_call(
        _norm_kernel,
        grid=(bsz, seqlen // ts),
        in_specs=[_row_spec(ts, d), _full_spec((1, d))],
        out_specs=_row_spec(ts, d),
        out_shape=jax.ShapeDtypeStruct((bsz, seqlen, d), BF16),
        compiler_params=_params(2),
        name="first_norm",
    )(x, w.reshape(1, d))


CHUNKS_PER_ITERATION = 2


def _interleave(*pieces):
    live = list(pieces)
    while live:
        for p in list(live):
            try:
                next(p)
            except StopIteration:
                live.remove(p)


def _pipelined_chunks(n_chunks, decay_pieces, score_pieces, tail_pieces):
    assert n_chunks % 2 == 0 and n_chunks >= 4 and CHUNKS_PER_ITERATION % 2 == 0

    def step(c, slot):
        _interleave(score_pieces(c, slot), decay_pieces(c + 1, 1 - slot),
                    tail_pieces(c - 1, 1 - slot))

    _interleave(decay_pieces(0, 0))
    _interleave(decay_pieces(1, 1), score_pieces(0, 0))
    n_loop = (n_chunks - 2) // CHUNKS_PER_ITERATION

    def body(i, carry):
        first = CHUNKS_PER_ITERATION * i + 1
        for k in range(CHUNKS_PER_ITERATION):
            step(first + k, (1 + k) % 2)
        return carry

    lax.fori_loop(0, n_loop, body, 0)
    for c in range(1 + n_loop * CHUNKS_PER_ITERATION, n_chunks - 1):
        step(c, c % 2)
    _interleave(score_pieces(n_chunks - 1, 1), tail_pieces(n_chunks - 2, 0))
    _interleave(tail_pieces(n_chunks - 1, 1))


def _chunk_rows(ci):
    start = ci * CHUNK
    if not isinstance(ci, int):
        start = pl.multiple_of(start, CHUNK)
    return pl.ds(start, CHUNK)


def _decay_pieces(log_decay, dmat_ref, d_ref, e_ref, elast_ref):
    c = CHUNK
    width = d_ref.shape[1]
    n_matmul_levels = dmat_ref.shape[0] // c - 1
    vpu_levels = LEVELS[:len(LEVELS) - n_matmul_levels]
    matmul_levels = LEVELS[len(LEVELS) - n_matmul_levels:]
    assert all(2 * s % SUBLANES == 0 for s in vpu_levels)
    d_ref[...] = _dot(dmat_ref[...], _split_bf16(log_decay * LOG2E))
    yield

    def put(blk, exponent):
        ex = jnp.exp2(exponent)
        e_ref[blk * c:(blk + 1) * c, :] = ex.astype(BF16)
        return ex

    elast_ref[...] = put(0, d_ref[0:c, :])[c - SUBLANES:c, :]
    yield
    b = d_ref[0:c, :]
    put(1, b[c - 1:c, :] - b)
    yield
    for s in vpu_levels:
        b = d_ref[0:c, :]
        ref = jnp.concatenate(
            [jnp.broadcast_to(b[g + s - 1:g + s, :], (2 * s, width)) for g in range(0, c, 2 * s)],
            axis=0)
        put(2 + LEVELS.index(s), -jnp.abs(b - ref))
        yield
    for n, s in enumerate(matmul_levels):
        put(2 + LEVELS.index(s), d_ref[(1 + n) * c:(2 + n) * c, :])
        yield


def _score_pieces(q_units, k_units, v_heads, decay_units, heads_per_unit, masks_ref, e_ref,
                  elast_ref, state_ref, o_ref):
    c = CHUNK
    n_heads = len(v_heads)
    lane = lax.broadcasted_iota(jnp.int32, (c, LANES), 1)
    head_w = LANES // heads_per_unit
    unit_lanes = [slice(u * LANES, (u + 1) * LANES) for u in range(len(q_units))]
    k16_units = [k.astype(BF16) for k in k_units]
    qf_heads, q16_heads, kown_heads = [], [], []
    for h in range(n_heads):
        u = h // heads_per_unit
        qf = q_units[u]
        k16 = k16_units[u]
        if heads_per_unit > 1:
            sub = h % heads_per_unit
            in_head = (lane >= sub * head_w) & (lane < (sub + 1) * head_w)
            qf = jnp.where(in_head, qf, 0.0)
            kown_heads.append(jnp.where(in_head, k16, jnp.zeros_like(k16)))
        else:
            kown_heads.append(k16)
        qf_heads.append(qf)
        q16_heads.append(qf.astype(BF16))
    v16_heads = [v.astype(BF16) for v in v_heads]
    yield

    odd_row = jnp.bitwise_and(lax.broadcasted_iota(jnp.int32, (c, 1), 0), 1) == 1
    k_prev_units = [pltpu.roll(k, 1, axis=0) for k in k_units]
    fine = []
    for h in range(n_heads):
        u = h // heads_per_unit
        s_diag = jnp.sum(qf_heads[h] * k_units[u], axis=-1, keepdims=True)
        s_pair = jnp.sum(qf_heads[h] * decay_units[u] * k_prev_units[u], axis=-1, keepdims=True)
        s_pair = jnp.where(odd_row, s_pair, 0.0)
        fine.append(s_diag * v_heads[h] + s_pair * pltpu.roll(v_heads[h], 1, axis=0))
    yield

    scores = [None] * n_heads
    for lvl in range(len(LEVELS)):
        for u, ul in enumerate(unit_lanes):
            e = e_ref[(2 + lvl) * c:(3 + lvl) * c, ul]
            unit_heads = range(u * heads_per_unit, (u + 1) * heads_per_unit)
            q_rows = jnp.concatenate([q16_heads[h] * e for h in unit_heads], axis=0)
            prod = _dot_nt(q_rows, k16_units[u] * e)
            for n, h in enumerate(unit_heads):
                term = masks_ref[lvl] * prod[n * c:(n + 1) * c, :]
                scores[h] = term if scores[h] is None else scores[h] + term
        yield

    states = [state_ref[h] for h in range(n_heads)]
    for h in range(n_heads):
        e_in = e_ref[0:c, unit_lanes[h // heads_per_unit]]
        o_ref[:, h * LANES:(h + 1) * LANES] = (
            _dot(scores[h].astype(BF16), v16_heads[h])
            + _dot(q16_heads[h] * e_in, states[h].astype(BF16)) + fine[h])
    yield
    for h in range(n_heads):
        lanes = unit_lanes[h // heads_per_unit]
        e_out = e_ref[c:2 * c, lanes]
        e_last = elast_ref[SUBLANES - 1:SUBLANES, lanes]
        e_col = jnp.transpose(jnp.broadcast_to(e_last, (LANES, LANES)))
        state_ref[h] = states[h] * e_col + _dot_tn(kown_heads[h] * e_out, v16_heads[h])
    yield


def _head_norm_gate(o, norm_w, gate):
    y = o * lax.rsqrt(jnp.mean(o * o, axis=-1, keepdims=True) + NORM_EPS) * norm_w
    return y * _silu(gate)


def _attention_scratch(ts, n_cols, decay_w, n_heads, dmat_rows):
    return [pltpu.VMEM((ts, n_cols), F32),
            pltpu.VMEM((dmat_rows, decay_w), F32),
            pltpu.VMEM((2, N_DECAY_BLOCKS * CHUNK, decay_w), BF16),
            pltpu.VMEM((2, SUBLANES, decay_w), F32),
            pltpu.VMEM((2, CHUNK, decay_w), F32),
            pltpu.VMEM((2, CHUNK, n_heads * LANES), F32),
            pltpu.VMEM((n_heads, LANES, LANES), F32)]


def _hgrn2_kernel(layer, z_ref, w_ref, lbl_ref, nw_ref, dmat_ref, masks_ref, y_ref,
                  proj_ref, d_ref, e_ref, elast_ref, dec_ref, o_ref, state_ref, key_ref):
    @pl.when(pl.program_id(1) == 0)
    def _():
        state_ref[...] = jnp.zeros_like(state_ref)

    proj_ref[...] = _dot_nt(z_ref[0], w_ref[...])

    logits = [lbl_ref[i:i + 1, :] for i in range(DEPTH)]
    mx = functools.reduce(jnp.maximum, logits)
    ex = [jnp.exp(t - mx) for t in logits]
    tot = functools.reduce(lambda a, b: a + b, ex)
    lb = functools.reduce(lambda a, b: a + b, ex[:layer + 1]) / tot - ex[0] / tot
    nw = nw_ref[...]
    w = BRANCH_W
    ts = proj_ref.shape[0]
    units = [slice(h * LANES, (h + 1) * LANES) for h in range(HG_HEADS)]

    def decay_pieces(ci, slot):
        rows = _chunk_rows(ci)
        zf = proj_ref[rows, w:2 * w]
        en = jnp.exp(-jnp.abs(zf))
        r = 1.0 / (1.0 + en)
        er = en * r
        pos = zf >= 0.0
        sig = jnp.where(pos, r, er)
        nsig = jnp.where(pos, er, r)
        key_ref[slot] = (1.0 - lb) * nsig
        forget = lb + (1.0 - lb) * sig
        dec_ref[slot] = forget
        log_f = jnp.log(forget)
        yield
        yield from _decay_pieces(log_f, dmat_ref, d_ref, e_ref.at[slot], elast_ref.at[slot])

    def score_pieces(ci, slot):
        rows = _chunk_rows(ci)
        q = proj_ref[rows, 0:w] * (HG_DK ** -0.5)
        key = key_ref[slot]
        forget = dec_ref[slot]
        vin = proj_ref[rows, 2 * w:3 * w]
        yield from _score_pieces([q[:, u] for u in units], [key[:, u] for u in units],
                                 [vin[:, u] for u in units], [forget[:, u] for u in units], 1,
                                 masks_ref, e_ref.at[slot], elast_ref.at[slot], state_ref,
                                 o_ref.at[slot])

    def tail_pieces(ci, slot):
        rows = _chunk_rows(ci)
        for h, u in enumerate(units):
            gate = proj_ref[rows, 3 * w + h * LANES:3 * w + (h + 1) * LANES]
            y_ref[0, rows, u] = _head_norm_gate(o_ref[slot, :, u], nw, gate).astype(BF16)
            yield

    _pipelined_chunks(ts // CHUNK, decay_pieces, score_pieces, tail_pieces)


def _hgrn2(z, w_all, lb_logits, norm_w, dmat, masks, layer, ts):
    bsz, seqlen, d = z.shape
    n_cols = _w_cols("hgrn2")
    return pl.pallas_call(
        functools.partial(_hgrn2_kernel, layer),
        grid=(bsz, seqlen // ts),
        in_specs=[_row_spec(ts, d), _w_spec("hgrn2", layer, d), _full_spec(lb_logits.shape),
                  _full_spec((1, LANES)), _full_spec(dmat.shape), _full_spec(masks.shape)],
        out_specs=_row_spec(ts, BRANCH_W),
        out_shape=jax.ShapeDtypeStruct((bsz, seqlen, BRANCH_W), BF16),
        scratch_shapes=_attention_scratch(ts, n_cols, BRANCH_W, HG_HEADS, dmat.shape[0])
                       + [pltpu.VMEM((2, CHUNK, BRANCH_W), F32)],
        compiler_params=_params(2),
        name="hgrn2_mixer",
    )(z, w_all, lb_logits, norm_w.reshape(1, LANES), dmat, masks)


def _gla_kernel(z_ref, w_ref, gkw_ref, gkb_ref, nw_ref, dmat_ref, masks_ref, y_ref,
                proj_ref, d_ref, e_ref, elast_ref, dec_ref, o_ref, state_ref):
    @pl.when(pl.program_id(1) == 0)
    def _():
        state_ref[...] = jnp.zeros_like(state_ref)

    proj_ref[...] = _dot_nt(z_ref[0], w_ref[0:proj_ref.shape[1], :])
    nw = nw_ref[...]
    gkb = gkb_ref[...]
    qk_w = GLA_HEADS * GLA_DK
    w = BRANCH_W
    ts = proj_ref.shape[0]
    units = [slice(u * LANES, (u + 1) * LANES) for u in range(qk_w // LANES)]
    heads = [slice(h * LANES, (h + 1) * LANES) for h in range(GLA_HEADS)]
    gate_off = 2 * qk_w + w

    def decay_pieces(ci, slot):
        rows = _chunk_rows(ci)
        lowrank = proj_ref[rows, gate_off + w:gate_off + w + LANES]
        gk = _dot(lowrank.astype(BF16), gkw_ref[...]) + gkb
        log_alpha = (jnp.minimum(gk, 0.0) - jnp.log(1.0 + jnp.exp(-jnp.abs(gk)))) * (1.0 / GLA_TAU)
        dec_ref[slot] = jnp.exp(log_alpha)
        yield
        yield from _decay_pieces(log_alpha, dmat_ref, d_ref, e_ref.at[slot], elast_ref.at[slot])

    def score_pieces(ci, slot):
        rows = _chunk_rows(ci)
        q = proj_ref[rows, 0:qk_w] * (GLA_DK ** -0.5)
        k = proj_ref[rows, qk_w:2 * qk_w]
        v = proj_ref[rows, 2 * qk_w:2 * qk_w + w]
        alpha = dec_ref[slot]
        yield from _score_pieces([q[:, u] for u in units], [k[:, u] for u in units],
                                 [v[:, hs] for hs in heads], [alpha[:, u] for u in units],
                                 LANES // GLA_DK, masks_ref, e_ref.at[slot], elast_ref.at[slot],
                                 state_ref, o_ref.at[slot])

    def tail_pieces(ci, slot):
        rows = _chunk_rows(ci)
        for h, hs in enumerate(heads):
            gate = proj_ref[rows, gate_off + h * LANES:gate_off + (h + 1) * LANES]
            y_ref[0, rows, hs] = _head_norm_gate(o_ref[slot, :, hs], nw, gate).astype(BF16)
            yield

    _pipelined_chunks(ts // CHUNK, decay_pieces, score_pieces, tail_pieces)


def _gla(z, w_all, gk_w, gk_b, norm_w, dmat, masks, layer, ts):
    bsz, seqlen, d = z.shape
    n_cols = _w_cols("gla")
    qk_w = GLA_HEADS * GLA_DK
    return pl.pallas_call(
        _gla_kernel,
        grid=(bsz, seqlen // ts),
        in_specs=[_row_spec(ts, d), _w_spec("gla", layer, d), _full_spec((LANES, qk_w)),
                  _full_spec((1, qk_w)), _full_spec((1, LANES)), _full_spec(dmat.shape),
                  _full_spec(masks.shape)],
        out_specs=_row_spec(ts, BRANCH_W),
        out_shape=jax.ShapeDtypeStruct((bsz, seqlen, BRANCH_W), BF16),
        scratch_shapes=_attention_scratch(ts, n_cols, qk_w, GLA_HEADS, dmat.shape[0]),
        compiler_params=_params(2),
        name="gla_mixer",
    )(z, w_all, gk_w, gk_b.reshape(1, qk_w), norm_w.reshape(1, LANES), dmat, masks)


def _gelu(x):
    return 0.5 * x * (1.0 + lax.erf(x * (2.0 ** -0.5)))


def _sg_kernel(z_ref, w_ref, lnw_ref, lnb_ref, ws_ref, bias_ref, y_ref):
    w = BRANCH_W
    proj = _dot_nt(z_ref[0], w_ref[0:3 * w, :])
    ts = proj.shape[0]
    u = _gelu(proj[:, 0:w])
    vf = _gelu(proj[:, w:2 * w])
    gate = proj[:, 2 * w:3 * w]
    mu = jnp.mean(vf, axis=-1, keepdims=True)
    dv = vf - mu
    var = jnp.mean(dv * dv, axis=-1, keepdims=True)
    vn = (dv * lax.rsqrt(var + NORM_EPS) * lnw_ref[...] + lnb_ref[...]).astype(BF16)
    ri = lax.broadcasted_iota(jnp.int32, (SG_LEN, SG_LEN), 0)
    cj = lax.broadcasted_iota(jnp.int32, (SG_LEN, SG_LEN), 1)
    allowed = (cj < CHUNK) | (ri >= CHUNK)
    wm = [jnp.where(allowed, ws_ref[g], 0.0).astype(BF16) for g in range(SG_GROUPS)]
    bias = bias_ref[...]
    ug = u * _silu(gate)
    for blk in range(ts // SG_LEN):
        rows = slice(blk * SG_LEN, (blk + 1) * SG_LEN)
        for g in range(SG_GROUPS):
            cols = slice(g * LANES, (g + 1) * LANES)
            mixed = _dot(wm[g], vn[rows, cols]) + bias[:, cols]
            y_ref[0, rows, cols] = (ug[rows, cols] * mixed).astype(BF16)


def _spatial_gating(z, w_all, ln_w, ln_b, w_s, bias, layer, ts):
    bsz, seqlen, d = z.shape
    assert _w_cols("sg") == 3 * BRANCH_W
    return pl.pallas_call(
        _sg_kernel,
        grid=(bsz, seqlen // ts),
        in_specs=[_row_spec(ts, d), _w_spec("sg", layer, d), _full_spec((1, BRANCH_W)),
                  _full_spec((1, BRANCH_W)), _full_spec(w_s.shape), _full_spec(bias.shape)],
        out_specs=_row_spec(ts, BRANCH_W),
        out_shape=jax.ShapeDtypeStruct((bsz, seqlen, BRANCH_W), BF16),
        compiler_params=_params(2),
        name="spatial_gating_mixer",
    )(z, w_all, ln_w.reshape(1, BRANCH_W), ln_b.reshape(1, BRANCH_W), w_s, bias)


def _lru_kernel(z_ref, w_ref, cw_ref, cb_ref, wax_ref, ba_ref, bx_ref, lam_ref, y_ref,
                xtb_ref, hs_ref, hprev_ref):
    bsz, tt, d = z_ref.shape
    w = BRANCH_W
    n_rows = bsz * tt
    n_slabs = w // LANES
    tail = (CONV_WIDTH - 1) * bsz

    @pl.when(pl.program_id(0) == 0)
    def _():
        xtb_ref[:, 0:tail, :] = jnp.zeros((n_slabs, tail, LANES), F32)
        hprev_ref[...] = jnp.zeros_like(hprev_ref)

    proj = _dot_nt(z_ref[...].reshape(n_rows, d), w_ref[...])
    gate = proj[:, w:2 * w]
    for b in range(bsz):
        rows = slice(b * tt, (b + 1) * tt)
        for j in range(n_slabs):
            xtb_ref[j, pl.ds(tail + b, tt, stride=bsz), :] = proj[rows, j * LANES:(j + 1) * LANES]

    def delayed(back):
        start = tail - back * bsz
        return jnp.concatenate(
            [xtb_ref[j, start:start + n_rows, :] for j in range(n_slabs)], axis=1)

    xc = cb_ref[...]
    for tap in range(CONV_WIDTH):
        xc = xc + delayed(CONV_WIDTH - 1 - tap) * cw_ref[tap:tap + 1, :]
    for j in range(n_slabs):
        xtb_ref[j, 0:tail, :] = xtb_ref[j, n_rows:n_rows + tail, :]

    xc16 = xc.astype(BF16)
    ra, ix = [], []
    for h in range(LRU_HEADS):
        cols = slice(h * LRU_HD, (h + 1) * LRU_HD)
        both = _dot(xc16[:, cols], wax_ref[h])
        ra.append(both[:, 0:LRU_HD])
        ix.append(both[:, LRU_HD:2 * LRU_HD])
    ig = _sigmoid(jnp.concatenate(ix, axis=1) + bx_ref[...])
    nlam = -lam_ref[...]
    softplus = jnp.maximum(nlam, 0.0) + jnp.log(1.0 + jnp.exp(-jnp.abs(nlam)))
    half_scale = (-0.5 * RG_C) * softplus
    log_a = half_scale + half_scale * jnp.tanh(0.5 * (jnp.concatenate(ra, axis=1) + ba_ref[...]))
    a = jnp.exp(log_a)
    var = jnp.maximum(-jnp.tanh(log_a) * (a * a + 1.0), SQRT_EPS)
    bt = (var * lax.rsqrt(var)) * (ig * xc)

    hcur = [hprev_ref[j] for j in range(n_slabs)]
    for t in range(tt):
        step = slice(t * bsz, (t + 1) * bsz)
        for j in range(n_slabs):
            cols = slice(j * LANES, (j + 1) * LANES)
            hcur[j] = a[step, cols] * hcur[j] + bt[step, cols]
            hs_ref[j, step, :] = hcur[j]
    for j in range(n_slabs):
        hprev_ref[j] = hcur[j]

    hseq = jnp.concatenate(
        [jnp.concatenate([hs_ref[j, pl.ds(b, tt, stride=bsz), :] for j in range(n_slabs)], axis=1)
         for b in range(bsz)], axis=0)
    y_ref[...] = (hseq * _silu(gate)).astype(BF16).reshape(bsz, tt, w)


def _rglru(z, w_all, conv_w, conv_b, wax, b_a, b_x, lam, layer, tt):
    bsz, seqlen, d = z.shape
    w = BRANCH_W
    assert _w_cols("lru") == 2 * w
    assert bsz == SUBLANES, "the RG-LRU scan puts the batch on the vreg sublanes"
    n_slabs = w // LANES
    return pl.pallas_call(
        _lru_kernel,
        grid=(seqlen // tt,),
        in_specs=[pl.BlockSpec((bsz, tt, d), lambda i: (0, i, 0)), _w_spec("lru", layer, d),
                  _full_spec((CONV_WIDTH, w)), _full_spec((1, w)), _full_spec(wax.shape),
                  _full_spec((1, w)), _full_spec((1, w)), _full_spec((1, w))],
        out_specs=pl.BlockSpec((bsz, tt, w), lambda i: (0, i, 0)),
        out_shape=jax.ShapeDtypeStruct((bsz, seqlen, w), BF16),
        scratch_shapes=[pltpu.VMEM((n_slabs, bsz * (tt + CONV_WIDTH - 1), LANES), F32),
                        pltpu.VMEM((n_slabs, bsz * tt, LANES), F32),
                        pltpu.VMEM((n_slabs, bsz, LANES), F32)],
        compiler_params=_params(1),
        name="rglru_mixer",
    )(z, w_all, conv_w, conv_b.reshape(1, w), wax, b_a.reshape(1, w), b_x.reshape(1, w),
      lam.reshape(1, w))


MERGE_COLS = 256


def _merge_kernel(final, z_ref, h_ref, ya_ref, yb_ref, yc_ref, yd_ref, wmg_ref, wbr_ref,
                  wout_ref, nw_ref, *rest):
    if final:
        out_ref, merged_ref = rest
    else:
        hn_ref, zn_ref, merged_ref = rest
    zt = z_ref[0]
    ys = [ya_ref[0], yb_ref[0], yc_ref[0], yd_ref[0]]
    for n in range(D_MODEL // MERGE_COLS):
        cols = slice(n * MERGE_COLS, (n + 1) * MERGE_COLS)
        merged = None
        for b in range(N_BRANCH):
            gcols = slice(b * D_MODEL + n * MERGE_COLS, b * D_MODEL + (n + 1) * MERGE_COLS)
            gate = _sigmoid(_dot_nt(zt, wmg_ref[gcols, :]))
            term = gate * _dot(ys[b], wbr_ref[b, :, cols])
            merged = term if merged is None else merged + term
        merged_ref[:, cols] = merged.astype(BF16)
    out = h_ref[0] + _dot(merged_ref[...], wout_ref[...])
    normed = _rms_norm_rows(out, nw_ref[...])
    if final:
        out_ref[0] = normed
    else:
        hn_ref[0] = out
        zn_ref[0] = normed.astype(BF16)


def _merge(z, h, ys, w_all, w_br, w_out, next_norm_w, layer, final, ts):
    bsz, seqlen, d = h.shape
    assert _w_cols("merge") == N_BRANCH * d
    f32_out = jax.ShapeDtypeStruct((bsz, seqlen, d), F32)
    if final:
        out_shape, out_specs = f32_out, _row_spec(ts, d)
    else:
        out_shape = (f32_out, jax.ShapeDtypeStruct((bsz, seqlen, d), BF16))
        out_specs = (_row_spec(ts, d), _row_spec(ts, d))
    return pl.pallas_call(
        functools.partial(_merge_kernel, final),
        grid=(bsz, seqlen // ts),
        in_specs=[_row_spec(ts, d), _row_spec(ts, d)] + [_row_spec(ts, BRANCH_W)] * N_BRANCH
                 + [_w_spec("merge", layer, d),
                    pl.BlockSpec((None,) + w_br.shape[1:], lambda *_: (layer, 0, 0, 0),
                                 pipeline_mode=pl.Buffered(1)),
                    pl.BlockSpec((None,) + w_out.shape[1:], lambda *_: (layer, 0, 0),
                                 pipeline_mode=pl.Buffered(1)),
                    _full_spec((1, d))],
        out_specs=out_specs,
        out_shape=out_shape,
        scratch_shapes=[pltpu.VMEM((ts, d), BF16)],
        compiler_params=_params(2),
        name="merge_final" if final else "merge",
    )(z, h, *ys, w_all, w_br, w_out, next_norm_w.reshape(1, d))


HGRN2_MATMUL_LEVELS = 4
GLA_MATMUL_LEVELS = 2
TS_NORM = 1024
TS_ATTN = 2048
TS_SG = 1024
TT_LRU = 128
TS_MERGE = 512


def kernel(x, norm_w, w_in, hg_lb_logits, hg_norm_w, gla_gk_w, gla_gk_b, gla_norm_w, sg_ln_w, sg_ln_b, sg_w, sg_b, lru_conv_w, lru_conv_b, lru_w_a, lru_b_a, lru_w_x, lru_b_x, lru_lambda, w_branch, w_out, final_norm_w):
    dmat_hgrn2 = jnp.asarray(_decay_matrix(HGRN2_MATMUL_LEVELS), BF16)
    dmat_gla = jnp.asarray(_decay_matrix(GLA_MATMUL_LEVELS), BF16)
    masks = jnp.asarray(_level_masks(), F32)
    gkw_pad = jnp.zeros((LANES - GLA_RANK, GLA_HEADS * GLA_DK), BF16)
    w_all = _pack_in_proj(w_in)
    w_br = w_branch.astype(BF16)
    w_o = w_out.astype(BF16)

    z = _first_norm(x, norm_w[0], TS_NORM)
    h = x
    for l in range(DEPTH):
        gkw = jnp.concatenate([gla_gk_w[l].astype(BF16), gkw_pad], axis=0)
        wax = jnp.concatenate([lru_w_a[l], lru_w_x[l]], axis=-1).astype(BF16)
        sg_bias = jnp.repeat(sg_b[l].T, LANES, axis=1)

        y_a = _hgrn2(z, w_all, hg_lb_logits, hg_norm_w[l], dmat_hgrn2, masks, l, TS_ATTN)
        y_b = _gla(z, w_all, gkw, gla_gk_b[l], gla_norm_w[l], dmat_gla, masks, l, TS_ATTN)
        y_c = _spatial_gating(z, w_all, sg_ln_w[l], sg_ln_b[l], sg_w[l], sg_bias, l, TS_SG)
        y_d = _rglru(z, w_all, lru_conv_w[l], lru_conv_b[l], wax, lru_b_a[l], lru_b_x[l],
                     lru_lambda[l], l, TT_LRU)
        ys = (y_a, y_b, y_c, y_d)
        if l + 1 < DEPTH:
            h, z = _merge(z, h, ys, w_all, w_br, w_o, norm_w[l + 1], l, False, TS_MERGE)
        else:
            return _merge(z, h, ys, w_all, w_br, w_o, final_norm_w, l, True, TS_MERGE)
```

```python
import functools

import numpy as np
import jax
import jax.numpy as jnp
from jax import lax
from jax.experimental import pallas as pl
from jax.experimental.pallas import tpu as pltpu

F32 = jnp.float32
BF16 = jnp.bfloat16

D_MODEL = 1024
DEPTH = 2
CHUNK = 64
BRANCH_W = 512
N_BRANCH = 4
NORM_EPS = 1e-6
SQRT_EPS = 1e-12
HG_HEADS = 4
HG_DK = 128
GLA_HEADS = 4
GLA_DK = 64
GLA_RANK = 16
GLA_TAU = 16.0
SG_GROUPS = 4
SG_LEN = 128
LRU_HEADS = 4
LRU_HD = 128
CONV_WIDTH = 4
RG_C = 8.0

LANES = 128
SUBLANES = 8
VMEM_LIMIT_BYTES = 56 * 1024 * 1024
LOG2E = 1.4426950408889634

OFF_A = 0
OFF_B = 2048
OFF_LR = 3584
OFF_C = 3600
OFF_D = 5136
OFF_MG = 6160
N_IN = 10256

LEVELS = (32, 16, 8, 4, 2)
N_DECAY_BLOCKS = 2 + len(LEVELS)
SPLIT_TERMS = 3


def _decay_matrix(n_matmul_levels):
    c = CHUNK
    i = np.arange(c)[:, None]
    j = np.arange(c)[None, :]
    blocks = [(j <= i)]
    for s in LEVELS[len(LEVELS) - n_matmul_levels:]:
        ref = (i // (2 * s)) * (2 * s) + s - 1
        in_b = i > ref
        blocks.append(np.where(in_b, (j > ref) & (j <= i), (j > i) & (j <= ref)))
    m = np.concatenate(blocks, axis=0).astype(np.float32)
    return np.concatenate([m] * SPLIT_TERMS, axis=1)


def _level_masks():
    c = CHUNK
    i = np.arange(c)[:, None]
    j = np.arange(c)[None, :]
    masks = []
    for s in LEVELS:
        same = (i // (2 * s)) == (j // (2 * s))
        masks.append(same & ((i // s) % 2 == 1) & ((j // s) % 2 == 0))
    return np.stack(masks).astype(np.float32)


def _sigmoid(x):
    return 0.5 + 0.5 * jnp.tanh(0.5 * x)


def _silu(x):
    half = 0.5 * x
    return half + half * jnp.tanh(half)


def _dot(a, b):
    return jnp.dot(a, b, preferred_element_type=F32)


def _dot_nt(a, b):
    return lax.dot_general(a, b, (((1,), (1,)), ((), ())), preferred_element_type=F32)


def _dot_tn(a, b):
    return lax.dot_general(a, b, (((0,), (0,)), ((), ())), preferred_element_type=F32)


def _split_bf16(x):
    terms = []
    rest = x
    for _ in range(SPLIT_TERMS):
        t = rest.astype(BF16)
        terms.append(t)
        rest = rest - t.astype(F32)
    return jnp.concatenate(terms, axis=0)


def _rms_norm_rows(x, w):
    return x * lax.rsqrt(jnp.mean(x * x, axis=-1, keepdims=True) + NORM_EPS) * w


def _params(n_grid_dims):
    return pltpu.CompilerParams(
        dimension_semantics=("arbitrary",) * n_grid_dims,
        vmem_limit_bytes=VMEM_LIMIT_BYTES,
    )


def _full_spec(shape):
    return pl.BlockSpec(shape, lambda *_: (0,) * len(shape))


def _row_spec(ts, width):
    return pl.BlockSpec((1, ts, width), lambda b, s: (b, s, 0))


_W_ROWS = {"hgrn2": (OFF_A, OFF_B - OFF_A), "gla": (OFF_B, OFF_LR - OFF_B + LANES),
           "sg": (OFF_C, OFF_D - OFF_C), "lru": (OFF_D, OFF_MG - OFF_D),
           "merge": (OFF_MG, N_IN - OFF_MG)}


def _transposed_in_proj(w_in):
    return jnp.swapaxes(w_in, 1, 2).astype(BF16)


def _w_spec(name, layer, d):
    start, rows = _W_ROWS[name]
    assert start + rows <= N_IN
    return pl.BlockSpec((pl.Element(1), pl.Element(rows), pl.Element(d)),
                        lambda *_: (layer, start, 0), pipeline_mode=pl.Buffered(1))


def _w_cols(name):
    return _W_ROWS[name][1]


def _norm_kernel(x_ref, w_ref, z_ref):
    z_ref[0] = _rms_norm_rows(x_ref[0], w_ref[...]).astype(BF16)


def _first_norm(x, w, ts):
    bsz, seqlen, d = x.shape
    return pl.pallas_call(
        _norm_kernel,
        grid=(bsz, seqlen // ts),
        in_specs=[_row_spec(ts, d), _full_spec((1, d))],
        out_specs=_row_spec(ts, d),
        out_shape=jax.ShapeDtypeStruct((bsz, seqlen, d), BF16),
        compiler_params=_params(2),
        name="first_norm",
    )(x, w.reshape(1, d))


CHUNKS_PER_ITERATION = 2


def _interleave(*pieces):
    live = list(pieces)
    while live:
        for p in list(live):
            try:
                next(p)
            except StopIteration:
                live.remove(p)


def _pipelined_chunks(n_chunks, decay_pieces, score_pieces, tail_pieces):
    assert n_chunks % 2 == 0 and n_chunks >= 4 and CHUNKS_PER_ITERATION % 2 == 0

    def step(c, slot):
        _interleave(score_pieces(c, slot), decay_pieces(c + 1, 1 - slot),
                    tail_pieces(c - 1, 1 - slot))

    _interleave(decay_pieces(0, 0))
    _interleave(decay_pieces(1, 1), score_pieces(0, 0))
    n_loop = (n_chunks - 2) // CHUNKS_PER_ITERATION

    def body(i, carry):
        first = CHUNKS_PER_ITERATION * i + 1
        for k in range(CHUNKS_PER_ITERATION):
            step(first + k, (1 + k) % 2)
        return carry

    lax.fori_loop(0, n_loop, body, 0)
    for c in range(1 + n_loop * CHUNKS_PER_ITERATION, n_chunks - 1):
        step(c, c % 2)
    _interleave(score_pieces(n_chunks - 1, 1), tail_pieces(n_chunks - 2, 0))
    _interleave(tail_pieces(n_chunks - 1, 1))


def _chunk_rows(ci):
    start = ci * CHUNK
    if not isinstance(ci, int):
        start = pl.multiple_of(start, CHUNK)
    return pl.ds(start, CHUNK)


def _decay_pieces(log_decay, dmat_ref, d_ref, e_ref, elast_ref):
    c = CHUNK
    width = d_ref.shape[1]
    n_matmul_levels = dmat_ref.shape[0] // c - 1
    vpu_levels = LEVELS[:len(LEVELS) - n_matmul_levels]
    matmul_levels = LEVELS[len(LEVELS) - n_matmul_levels:]
    assert all(2 * s % SUBLANES == 0 for s in vpu_levels)
    d_ref[...] = _dot(dmat_ref[...], _split_bf16(log_decay * LOG2E))
    yield

    def put(blk, exponent):
        ex = jnp.exp2(exponent)
        e_ref[blk * c:(blk + 1) * c, :] = ex.astype(BF16)
        return ex

    elast_ref[...] = put(0, d_ref[0:c, :])[c - SUBLANES:c, :]
    yield
    b = d_ref[0:c, :]
    put(1, b[c - 1:c, :] - b)
    yield
    for s in vpu_levels:
        b = d_ref[0:c, :]
        ref = jnp.concatenate(
            [jnp.broadcast_to(b[g + s - 1:g + s, :], (2 * s, width)) for g in range(0, c, 2 * s)],
            axis=0)
        put(2 + LEVELS.index(s), -jnp.abs(b - ref))
        yield
    for n, s in enumerate(matmul_levels):
        put(2 + LEVELS.index(s), d_ref[(1 + n) * c:(2 + n) * c, :])
        yield


def _score_pieces(q_units, k_units, v_heads, decay_units, heads_per_unit, masks_ref, e_ref,
                  elast_ref, state_ref, o_ref):
    c = CHUNK
    n_heads = len(v_heads)
    lane = lax.broadcasted_iota(jnp.int32, (c, LANES), 1)
    head_w = LANES // heads_per_unit
    unit_lanes = [slice(u * LANES, (u + 1) * LANES) for u in range(len(q_units))]
    k16_units = [k.astype(BF16) for k in k_units]
    qf_heads, q16_heads, kown_heads = [], [], []
    for h in range(n_heads):
        u = h // heads_per_unit
        qf = q_units[u]
        k16 = k16_units[u]
        if heads_per_unit > 1:
            sub = h % heads_per_unit
            in_head = (lane >= sub * head_w) & (lane < (sub + 1) * head_w)
            qf = jnp.where(in_head, qf, 0.0)
            kown_heads.append(jnp.where(in_head, k16, jnp.zeros_like(k16)))
        else:
            kown_heads.append(k16)
        qf_heads.append(qf)
        q16_heads.append(qf.astype(BF16))
    v16_heads = [v.astype(BF16) for v in v_heads]
    yield

    odd_row = jnp.bitwise_and(lax.broadcasted_iota(jnp.int32, (c, 1), 0), 1) == 1
    k_prev_units = [pltpu.roll(k, 1, axis=0) for k in k_units]
    fine = []
    for h in range(n_heads):
        u = h // heads_per_unit
        s_diag = jnp.sum(qf_heads[h] * k_units[u], axis=-1, keepdims=True)
        s_pair = jnp.sum(qf_heads[h] * decay_units[u] * k_prev_units[u], axis=-1, keepdims=True)
        s_pair = jnp.where(odd_row, s_pair, 0.0)
        fine.append(s_diag * v_heads[h] + s_pair * pltpu.roll(v_heads[h], 1, axis=0))
    yield

    scores = [None] * n_heads
    for lvl in range(len(LEVELS)):
        for u, ul in enumerate(unit_lanes):
            e = e_ref[(2 + lvl) * c:(3 + lvl) * c, ul]
            unit_heads = range(u * heads_per_unit, (u + 1) * heads_per_unit)
            q_rows = jnp.concatenate([q16_heads[h] * e for h in unit_heads], axis=0)
            prod = _dot_nt(q_rows, k16_units[u] * e)
            for n, h in enumerate(unit_heads):
                term = masks_ref[lvl] * prod[n * c:(n + 1) * c, :]
                scores[h] = term if scores[h] is None else scores[h] + term
        yield

    states = [state_ref[h] for h in range(n_heads)]
    for h in range(n_heads):
        e_in = e_ref[0:c, unit_lanes[h // heads_per_unit]]
        o_ref[:, h * LANES:(h + 1) * LANES] = (
            _dot(scores[h].astype(BF16), v16_heads[h])
            + _dot(q16_heads[h] * e_in, states[h].astype(BF16)) + fine[h])
    yield
    for h in range(n_heads):
        lanes = unit_lanes[h // heads_per_unit]
        e_out = e_ref[c:2 * c, lanes]
        e_last = elast_ref[SUBLANES - 1:SUBLANES, lanes]
        e_col = jnp.transpose(jnp.broadcast_to(e_last, (LANES, LANES)))
        state_ref[h] = states[h] * e_col + _dot_tn(kown_heads[h] * e_out, v16_heads[h])
    yield


def _head_norm_gate(o, norm_w, gate):
    y = o * lax.rsqrt(jnp.mean(o * o, axis=-1, keepdims=True) + NORM_EPS) * norm_w
    return y * _silu(gate)


def _attention_scratch(ts, n_cols, decay_w, n_heads, dmat_rows):
    return [pltpu.VMEM((ts, n_cols), F32),
            pltpu.VMEM((dmat_rows, decay_w), F32),
            pltpu.VMEM((2, N_DECAY_BLOCKS * CHUNK, decay_w), BF16),
            pltpu.VMEM((2, SUBLANES, decay_w), F32),
            pltpu.VMEM((2, CHUNK, decay_w), F32),
            pltpu.VMEM((2, CHUNK, n_heads * LANES), F32),
            pltpu.VMEM((n_heads, LANES, LANES), F32)]


def _hgrn2_kernel(layer, z_ref, w_ref, lbl_ref, nw_ref, dmat_ref, masks_ref, y_ref,
                  proj_ref, d_ref, e_ref, elast_ref, dec_ref, o_ref, state_ref, key_ref):
    @pl.when(pl.program_id(1) == 0)
    def _():
        state_ref[...] = jnp.zeros_like(state_ref)

    proj_ref[...] = _dot_nt(z_ref[0], w_ref[0])

    logits = [lbl_ref[i:i + 1, :] for i in range(DEPTH)]
    mx = functools.reduce(jnp.maximum, logits)
    ex = [jnp.exp(t - mx) for t in logits]
    tot = functools.reduce(lambda a, b: a + b, ex)
    lb = functools.reduce(lambda a, b: a + b, ex[:layer + 1]) / tot - ex[0] / tot
    nw = nw_ref[...]
    w = BRANCH_W
    ts = proj_ref.shape[0]
    units = [slice(h * LANES, (h + 1) * LANES) for h in range(HG_HEADS)]

    def decay_pieces(ci, slot):
        rows = _chunk_rows(ci)
        zf = proj_ref[rows, w:2 * w]
        en = jnp.exp(-jnp.abs(zf))
        r = 1.0 / (1.0 + en)
        er = en * r
        pos = zf >= 0.0
        sig = jnp.where(pos, r, er)
        nsig = jnp.where(pos, er, r)
        key_ref[slot] = (1.0 - lb) * nsig
        forget = lb + (1.0 - lb) * sig
        dec_ref[slot] = forget
        log_f = jnp.log(forget)
        yield
        yield from _decay_pieces(log_f, dmat_ref, d_ref, e_ref.at[slot], elast_ref.at[slot])

    def score_pieces(ci, slot):
        rows = _chunk_rows(ci)
        q = proj_ref[rows, 0:w] * (HG_DK ** -0.5)
        key = key_ref[slot]
        forget = dec_ref[slot]
        vin = proj_ref[rows, 2 * w:3 * w]
        yield from _score_pieces([q[:, u] for u in units], [key[:, u] for u in units],
                                 [vin[:, u] for u in units], [forget[:, u] for u in units], 1,
                                 masks_ref, e_ref.at[slot], elast_ref.at[slot], state_ref,
                                 o_ref.at[slot])

    def tail_pieces(ci, slot):
        rows = _chunk_rows(ci)
        for h, u in enumerate(units):
            gate = proj_ref[rows, 3 * w + h * LANES:3 * w + (h + 1) * LANES]
            y_ref[0, rows, u] = _head_norm_gate(o_ref[slot, :, u], nw, gate).astype(BF16)
            yield

    _pipelined_chunks(ts // CHUNK, decay_pieces, score_pieces, tail_pieces)


def _hgrn2(z, w_all, lb_logits, norm_w, dmat, masks, layer, ts):
    bsz, seqlen, d = z.shape
    n_cols = _w_cols("hgrn2")
    return pl.pallas_call(
        functools.partial(_hgrn2_kernel, layer),
        grid=(bsz, seqlen // ts),
        in_specs=[_row_spec(ts, d), _w_spec("hgrn2", layer, d), _full_spec(lb_logits.shape),
                  _full_spec((1, LANES)), _full_spec(dmat.shape), _full_spec(masks.shape)],
        out_specs=_row_spec(ts, BRANCH_W),
        out_shape=jax.ShapeDtypeStruct((bsz, seqlen, BRANCH_W), BF16),
        scratch_shapes=_attention_scratch(ts, n_cols, BRANCH_W, HG_HEADS, dmat.shape[0])
                       + [pltpu.VMEM((2, CHUNK, BRANCH_W), F32)],
        compiler_params=_params(2),
        name="hgrn2_mixer",
    )(z, w_all, lb_logits, norm_w.reshape(1, LANES), dmat, masks)


def _gla_kernel(z_ref, w_ref, gkw_ref, gkb_ref, nw_ref, dmat_ref, masks_ref, y_ref,
                proj_ref, d_ref, e_ref, elast_ref, dec_ref, o_ref, state_ref):
    @pl.when(pl.program_id(1) == 0)
    def _():
        state_ref[...] = jnp.zeros_like(state_ref)

    proj_ref[...] = _dot_nt(z_ref[0], w_ref[0])
    nw = nw_ref[...]
    gkb = gkb_ref[...]
    qk_w = GLA_HEADS * GLA_DK
    w = BRANCH_W
    ts = proj_ref.shape[0]
    units = [slice(u * LANES, (u + 1) * LANES) for u in range(qk_w // LANES)]
    heads = [slice(h * LANES, (h + 1) * LANES) for h in range(GLA_HEADS)]
    gate_off = 2 * qk_w + w

    def decay_pieces(ci, slot):
        rows = _chunk_rows(ci)
        lowrank = proj_ref[rows, gate_off + w:gate_off + w + LANES]
        gk = _dot(lowrank.astype(BF16), gkw_ref[...]) + gkb
        log_alpha = (jnp.minimum(gk, 0.0) - jnp.log(1.0 + jnp.exp(-jnp.abs(gk)))) * (1.0 / GLA_TAU)
        dec_ref[slot] = jnp.exp(log_alpha)
        yield
        yield from _decay_pieces(log_alpha, dmat_ref, d_ref, e_ref.at[slot], elast_ref.at[slot])

    def score_pieces(ci, slot):
        rows = _chunk_rows(ci)
        q = proj_ref[rows, 0:qk_w] * (GLA_DK ** -0.5)
        k = proj_ref[rows, qk_w:2 * qk_w]
        v = proj_ref[rows, 2 * qk_w:2 * qk_w + w]
        alpha = dec_ref[slot]
        yield from _score_pieces([q[:, u] for u in units], [k[:, u] for u in units],
                                 [v[:, hs] for hs in heads], [alpha[:, u] for u in units],
                                 LANES // GLA_DK, masks_ref, e_ref.at[slot], elast_ref.at[slot],
                                 state_ref, o_ref.at[slot])

    def tail_pieces(ci, slot):
        rows = _chunk_rows(ci)
        for h, hs in enumerate(heads):
            gate = proj_ref[rows, gate_off + h * LANES:gate_off + (h + 1) * LANES]
            y_ref[0, rows, hs] = _head_norm_gate(o_ref[slot, :, hs], nw, gate).astype(BF16)
            yield

    _pipelined_chunks(ts // CHUNK, decay_pieces, score_pieces, tail_pieces)


def _gla(z, w_all, gk_w, gk_b, norm_w, dmat, masks, layer, ts):
    bsz, seqlen, d = z.shape
    n_cols = _w_cols("gla")
    qk_w = GLA_HEADS * GLA_DK
    return pl.pallas_call(
        _gla_kernel,
        grid=(bsz, seqlen // ts),
        in_specs=[_row_spec(ts, d), _w_spec("gla", layer, d), _full_spec((LANES, qk_w)),
                  _full_spec((1, qk_w)), _full_spec((1, LANES)), _full_spec(dmat.shape),
                  _full_spec(masks.shape)],
        out_specs=_row_spec(ts, BRANCH_W),
        out_shape=jax.ShapeDtypeStruct((bsz, seqlen, BRANCH_W), BF16),
        scratch_shapes=_attention_scratch(ts, n_cols, qk_w, GLA_HEADS, dmat.shape[0]),
        compiler_params=_params(2),
        name="gla_mixer",
    )(z, w_all, gk_w, gk_b.reshape(1, qk_w), norm_w.reshape(1, LANES), dmat, masks)


def _gelu(x):
    return 0.5 * x * (1.0 + lax.erf(x * (2.0 ** -0.5)))


def _sg_kernel(z_ref, w_ref, lnw_ref, lnb_ref, ws_ref, bias_ref, y_ref):
    w = BRANCH_W
    proj = _dot_nt(z_ref[0], w_ref[0])
    ts = proj.shape[0]
    u = _gelu(proj[:, 0:w])
    vf = _gelu(proj[:, w:2 * w])
    gate = proj[:, 2 * w:3 * w]
    mu = jnp.mean(vf, axis=-1, keepdims=True)
    dv = vf - mu
    var = jnp.mean(dv * dv, axis=-1, keepdims=True)
    vn = (dv * lax.rsqrt(var + NORM_EPS) * lnw_ref[...] + lnb_ref[...]).astype(BF16)
    ri = lax.broadcasted_iota(jnp.int32, (SG_LEN, SG_LEN), 0)
    cj = lax.broadcasted_iota(jnp.int32, (SG_LEN, SG_LEN), 1)
    allowed = (cj < CHUNK) | (ri >= CHUNK)
    wm = [jnp.where(allowed, ws_ref[g], 0.0).astype(BF16) for g in range(SG_GROUPS)]
    bias = bias_ref[...]
    ug = u * _silu(gate)
    for blk in range(ts // SG_LEN):
        rows = slice(blk * SG_LEN, (blk + 1) * SG_LEN)
        for g in range(SG_GROUPS):
            cols = slice(g * LANES, (g + 1) * LANES)
            mixed = _dot(wm[g], vn[rows, cols]) + bias[:, cols]
            y_ref[0, rows, cols] = (ug[rows, cols] * mixed).astype(BF16)


def _spatial_gating(z, w_all, ln_w, ln_b, w_s, bias, layer, ts):
    bsz, seqlen, d = z.shape
    assert _w_cols("sg") == 3 * BRANCH_W
    return pl.pallas_call(
        _sg_kernel,
        grid=(bsz, seqlen // ts),
        in_specs=[_row_spec(ts, d), _w_spec("sg", layer, d), _full_spec((1, BRANCH_W)),
                  _full_spec((1, BRANCH_W)), _full_spec(w_s.shape), _full_spec(bias.shape)],
        out_specs=_row_spec(ts, BRANCH_W),
        out_shape=jax.ShapeDtypeStruct((bsz, seqlen, BRANCH_W), BF16),
        compiler_params=_params(2),
        name="spatial_gating_mixer",
    )(z, w_all, ln_w.reshape(1, BRANCH_W), ln_b.reshape(1, BRANCH_W), w_s, bias)


def _lru_kernel(z_ref, w_ref, cw_ref, cb_ref, wax_ref, ba_ref, bx_ref, lam_ref, y_ref,
                xtb_ref, hs_ref, hprev_ref):
    bsz, tt, d = z_ref.shape
    w = BRANCH_W
    n_rows = bsz * tt
    n_slabs = w // LANES
    tail = (CONV_WIDTH - 1) * bsz

    @pl.when(pl.program_id(0) == 0)
    def _():
        xtb_ref[:, 0:tail, :] = jnp.zeros((n_slabs, tail, LANES), F32)
        hprev_ref[...] = jnp.zeros_like(hprev_ref)

    proj = _dot_nt(z_ref[...].reshape(n_rows, d), w_ref[0])
    gate = proj[:, w:2 * w]
    for b in range(bsz):
        rows = slice(b * tt, (b + 1) * tt)
        for j in range(n_slabs):
            xtb_ref[j, pl.ds(tail + b, tt, stride=bsz), :] = proj[rows, j * LANES:(j + 1) * LANES]

    def delayed(back):
        start = tail - back * bsz
        return jnp.concatenate(
            [xtb_ref[j, start:start + n_rows, :] for j in range(n_slabs)], axis=1)

    xc = cb_ref[...]
    for tap in range(CONV_WIDTH):
        xc = xc + delayed(CONV_WIDTH - 1 - tap) * cw_ref[tap:tap + 1, :]
    for j in range(n_slabs):
        xtb_ref[j, 0:tail, :] = xtb_ref[j, n_rows:n_rows + tail, :]

    xc16 = xc.astype(BF16)
    ra, ix = [], []
    for h in range(LRU_HEADS):
        cols = slice(h * LRU_HD, (h + 1) * LRU_HD)
        both = _dot(xc16[:, cols], wax_ref[h])
        ra.append(both[:, 0:LRU_HD])
        ix.append(both[:, LRU_HD:2 * LRU_HD])
    ig = _sigmoid(jnp.concatenate(ix, axis=1) + bx_ref[...])
    nlam = -lam_ref[...]
    softplus = jnp.maximum(nlam, 0.0) + jnp.log(1.0 + jnp.exp(-jnp.abs(nlam)))
    half_scale = (-0.5 * RG_C) * softplus
    log_a = half_scale + half_scale * jnp.tanh(0.5 * (jnp.concatenate(ra, axis=1) + ba_ref[...]))
    a = jnp.exp(log_a)
    var = jnp.maximum(-jnp.tanh(log_a) * (a * a + 1.0), SQRT_EPS)
    bt = (var * lax.rsqrt(var)) * (ig * xc)

    hcur = [hprev_ref[j] for j in range(n_slabs)]
    for t in range(tt):
        step = slice(t * bsz, (t + 1) * bsz)
        for j in range(n_slabs):
            cols = slice(j * LANES, (j + 1) * LANES)
            hcur[j] = a[step, cols] * hcur[j] + bt[step, cols]
            hs_ref[j, step, :] = hcur[j]
    for j in range(n_slabs):
        hprev_ref[j] = hcur[j]

    hseq = jnp.concatenate(
        [jnp.concatenate([hs_ref[j, pl.ds(b, tt, stride=bsz), :] for j in range(n_slabs)], axis=1)
         for b in range(bsz)], axis=0)
    y_ref[...] = (hseq * _silu(gate)).astype(BF16).reshape(bsz, tt, w)


def _rglru(z, w_all, conv_w, conv_b, wax, b_a, b_x, lam, layer, tt):
    bsz, seqlen, d = z.shape
    w = BRANCH_W
    assert _w_cols("lru") == 2 * w
    assert bsz == SUBLANES, "the RG-LRU scan puts the batch on the vreg sublanes"
    n_slabs = w // LANES
    return pl.pallas_call(
        _lru_kernel,
        grid=(seqlen // tt,),
        in_specs=[pl.BlockSpec((bsz, tt, d), lambda i: (0, i, 0)), _w_spec("lru", layer, d),
                  _full_spec((CONV_WIDTH, w)), _full_spec((1, w)), _full_spec(wax.shape),
                  _full_spec((1, w)), _full_spec((1, w)), _full_spec((1, w))],
        out_specs=pl.BlockSpec((bsz, tt, w), lambda i: (0, i, 0)),
        out_shape=jax.ShapeDtypeStruct((bsz, seqlen, w), BF16),
        scratch_shapes=[pltpu.VMEM((n_slabs, bsz * (tt + CONV_WIDTH - 1), LANES), F32),
                        pltpu.VMEM((n_slabs, bsz * tt, LANES), F32),
                        pltpu.VMEM((n_slabs, bsz, LANES), F32)],
        compiler_params=_params(1),
        name="rglru_mixer",
    )(z, w_all, conv_w, conv_b.reshape(1, w), wax, b_a.reshape(1, w), b_x.reshape(1, w),
      lam.reshape(1, w))


MERGE_COLS = 256


def _merge_kernel(final, z_ref, h_ref, ya_ref, yb_ref, yc_ref, yd_ref, wmg_ref, wbr_ref,
                  wout_ref, nw_ref, *rest):
    if final:
        out_ref, merged_ref = rest
    else:
        hn_ref, zn_ref, merged_ref = rest
    zt = z_ref[0]
    ys = [ya_ref[0], yb_ref[0], yc_ref[0], yd_ref[0]]
    for n in range(D_MODEL // MERGE_COLS):
        cols = slice(n * MERGE_COLS, (n + 1) * MERGE_COLS)
        merged = None
        for b in range(N_BRANCH):
            gcols = slice(b * D_MODEL + n * MERGE_COLS, b * D_MODEL + (n + 1) * MERGE_COLS)
            gate = _sigmoid(_dot_nt(zt, wmg_ref[0, gcols, :]))
            term = gate * _dot(ys[b], wbr_ref[b, :, cols])
            merged = term if merged is None else merged + term
        merged_ref[:, cols] = merged.astype(BF16)
    out = h_ref[0] + _dot(merged_ref[...], wout_ref[...])
    normed = _rms_norm_rows(out, nw_ref[...])
    if final:
        out_ref[0] = normed
    else:
        hn_ref[0] = out
        zn_ref[0] = normed.astype(BF16)


def _merge(z, h, ys, w_all, w_br, w_out, next_norm_w, layer, final, ts):
    bsz, seqlen, d = h.shape
    assert _w_cols("merge") == N_BRANCH * d
    f32_out = jax.ShapeDtypeStruct((bsz, seqlen, d), F32)
    if final:
        out_shape, out_specs = f32_out, _row_spec(ts, d)
    else:
        out_shape = (f32_out, jax.ShapeDtypeStruct((bsz, seqlen, d), BF16))
        out_specs = (_row_spec(ts, d), _row_spec(ts, d))
    return pl.pallas_call(
        functools.partial(_merge_kernel, final),
        grid=(bsz, seqlen // ts),
        in_specs=[_row_spec(ts, d), _row_spec(ts, d)] + [_row_spec(ts, BRANCH_W)] * N_BRANCH
                 + [_w_spec("merge", layer, d),
                    pl.BlockSpec((None,) + w_br.shape[1:], lambda *_: (layer, 0, 0, 0),
                                 pipeline_mode=pl.Buffered(1)),
                    pl.BlockSpec((None,) + w_out.shape[1:], lambda *_: (layer, 0, 0),
                                 pipeline_mode=pl.Buffered(1)),
                    _full_spec((1, d))],
        out_specs=out_specs,
        out_shape=out_shape,
        scratch_shapes=[pltpu.VMEM((ts, d), BF16)],
        compiler_params=_params(2),
        name="merge_final" if final else "merge",
    )(z, h, *ys, w_all, w_br, w_out, next_norm_w.reshape(1, d))


HGRN2_MATMUL_LEVELS = 4
GLA_MATMUL_LEVELS = 2
TS_NORM = 1024
TS_ATTN = 2048
TS_SG = 1024
TT_LRU = 128
TS_MERGE = 512


def kernel(x, norm_w, w_in, hg_lb_logits, hg_norm_w, gla_gk_w, gla_gk_b, gla_norm_w, sg_ln_w, sg_ln_b, sg_w, sg_b, lru_conv_w, lru_conv_b, lru_w_a, lru_b_a, lru_w_x, lru_b_x, lru_lambda, w_branch, w_out, final_norm_w):
    dmat_hgrn2 = jnp.asarray(_decay_matrix(HGRN2_MATMUL_LEVELS), BF16)
    dmat_gla = jnp.asarray(_decay_matrix(GLA_MATMUL_LEVELS), BF16)
    masks = jnp.asarray(_level_masks(), F32)
    gkw_pad = jnp.zeros((LANES - GLA_RANK, GLA_HEADS * GLA_DK), BF16)
    w_all = _transposed_in_proj(w_in)
    w_br = w_branch.astype(BF16)
    w_o = w_out.astype(BF16)

    z = _first_norm(x, norm_w[0], TS_NORM)
    h = x
    for l in range(DEPTH):
        gkw = jnp.concatenate([gla_gk_w[l].astype(BF16), gkw_pad], axis=0)
        wax = jnp.concatenate([lru_w_a[l], lru_w_x[l]], axis=-1).astype(BF16)
        sg_bias = jnp.repeat(sg_b[l].T, LANES, axis=1)

        y_a = _hgrn2(z, w_all, hg_lb_logits, hg_norm_w[l], dmat_hgrn2, masks, l, TS_ATTN)
        y_b = _gla(z, w_all, gkw, gla_gk_b[l], gla_norm_w[l], dmat_gla, masks, l, TS_ATTN)
        y_c = _spatial_gating(z, w_all, sg_ln_w[l], sg_ln_b[l], sg_w[l], sg_bias, l, TS_SG)
        y_d = _rglru(z, w_all, lru_conv_w[l], lru_conv_b[l], wax, lru_b_a[l], lru_b_x[l],
                     lru_lambda[l], l, TT_LRU)
        ys = (y_a, y_b, y_c, y_d)
        if l + 1 < DEPTH:
            h, z = _merge(z, h, ys, w_all, w_br, w_o, norm_w[l + 1], l, False, TS_MERGE)
        else:
            return _merge(z, h, ys, w_all, w_br, w_o, final_norm_w, l, True, TS_MERGE)
```

```python
import functools

import numpy as np
import jax
import jax.numpy as jnp
from jax import lax
from jax.experimental import pallas as pl
from jax.experimental.pallas import tpu as pltpu

F32 = jnp.float32
BF16 = jnp.bfloat16

D_MODEL = 1024
DEPTH = 2
CHUNK = 64
BRANCH_W = 512
N_BRANCH = 4
NORM_EPS = 1e-6
SQRT_EPS = 1e-12
HG_HEADS = 4
HG_DK = 128
GLA_HEADS = 4
GLA_DK = 64
GLA_RANK = 16
GLA_TAU = 16.0
SG_GROUPS = 4
SG_LEN = 128
LRU_HEADS = 4
LRU_HD = 128
CONV_WIDTH = 4
RG_C = 8.0

LANES = 128
SUBLANES = 8
VMEM_LIMIT_BYTES = 56 * 1024 * 1024
LOG2E = 1.4426950408889634

OFF_A = 0
OFF_B = 2048
OFF_LR = 3584
OFF_C = 3600
OFF_D = 5136
OFF_MG = 6160
N_IN = 10256

LEVELS = (32, 16, 8, 4, 2)
N_DECAY_BLOCKS = 2 + len(LEVELS)
SPLIT_TERMS = 3


def _decay_matrix(n_matmul_levels):
    c = CHUNK
    i = np.arange(c)[:, None]
    j = np.arange(c)[None, :]
    blocks = [(j <= i)]
    for s in LEVELS[len(LEVELS) - n_matmul_levels:]:
        ref = (i // (2 * s)) * (2 * s) + s - 1
        in_b = i > ref
        blocks.append(np.where(in_b, (j > ref) & (j <= i), (j > i) & (j <= ref)))
    m = np.concatenate(blocks, axis=0).astype(np.float32)
    return np.concatenate([m] * SPLIT_TERMS, axis=1)


def _level_masks():
    c = CHUNK
    i = np.arange(c)[:, None]
    j = np.arange(c)[None, :]
    masks = []
    for s in LEVELS:
        same = (i // (2 * s)) == (j // (2 * s))
        masks.append(same & ((i // s) % 2 == 1) & ((j // s) % 2 == 0))
    return np.stack(masks).astype(np.float32)


def _sigmoid(x):
    return 0.5 + 0.5 * jnp.tanh(0.5 * x)


def _silu(x):
    half = 0.5 * x
    return half + half * jnp.tanh(half)


def _dot(a, b):
    return jnp.dot(a, b, preferred_element_type=F32)


def _dot_nt(a, b):
    return lax.dot_general(a, b, (((1,), (1,)), ((), ())), preferred_element_type=F32)


def _dot_tn(a, b):
    return lax.dot_general(a, b, (((0,), (0,)), ((), ())), preferred_element_type=F32)


def _split_bf16(x):
    terms = []
    rest = x
    for _ in range(SPLIT_TERMS):
        t = rest.astype(BF16)
        terms.append(t)
        rest = rest - t.astype(F32)
    return jnp.concatenate(terms, axis=0)


def _rms_norm_rows(x, w):
    return x * lax.rsqrt(jnp.mean(x * x, axis=-1, keepdims=True) + NORM_EPS) * w


def _params(n_grid_dims):
    return pltpu.CompilerParams(
        dimension_semantics=("arbitrary",) * n_grid_dims,
        vmem_limit_bytes=VMEM_LIMIT_BYTES,
    )


def _full_spec(shape):
    return pl.BlockSpec(shape, lambda *_: (0,) * len(shape))


def _row_spec(ts, width):
    return pl.BlockSpec((1, ts, width), lambda b, s: (b, s, 0))


_W_ROWS = {"hgrn2": (OFF_A, OFF_B - OFF_A), "gla": (OFF_B, OFF_LR - OFF_B + LANES),
           "sg": (OFF_C, OFF_D - OFF_C), "lru": (OFF_D, OFF_MG - OFF_D),
           "merge": (OFF_MG, N_IN - OFF_MG)}


def _transposed_in_proj(w_in):
    return jnp.swapaxes(w_in, 1, 2).astype(BF16)


def _w_spec(name, layer, d):
    start, rows = _W_ROWS[name]
    assert start + rows <= N_IN
    return pl.BlockSpec((pl.Element(1), pl.Element(rows), pl.Element(d)),
                        lambda *_: (layer, start, 0), pipeline_mode=pl.Buffered(1))


def _w_cols(name):
    return _W_ROWS[name][1]


def _norm_kernel(x_ref, w_ref, z_ref):
    z_ref[0] = _rms_norm_rows(x_ref[0], w_ref[...]).astype(BF16)


def _first_norm(x, w, ts):
    bsz, seqlen, d = x.shape
    return pl.pallas_call(
        _norm_kernel,
        grid=(bsz, seqlen // ts),
        in_specs=[_row_spec(ts, d), _full_spec((1, d))],
        out_specs=_row_spec(ts, d),
        out_shape=jax.ShapeDtypeStruct((bsz, seqlen, d), BF16),
        compiler_params=_params(2),
        name="first_norm",
    )(x, w.reshape(1, d))


CHUNKS_PER_ITERATION = 2


def _interleave(*pieces):
    live = list(pieces)
    while live:
        for p in list(live):
            try:
                next(p)
            except StopIteration:
                live.remove(p)


def _pipelined_chunks(n_chunks, decay_pieces, score_pieces, tail_pieces):
    assert n_chunks % 2 == 0 and n_chunks >= 4 and CHUNKS_PER_ITERATION % 2 == 0

    def step(c, slot):
        _interleave(score_pieces(c, slot), decay_pieces(c + 1, 1 - slot),
                    tail_pieces(c - 1, 1 - slot))

    _interleave(decay_pieces(0, 0))
    _interleave(decay_pieces(1, 1), score_pieces(0, 0))
    n_loop = (n_chunks - 2) // CHUNKS_PER_ITERATION

    def body(i, carry):
        first = CHUNKS_PER_ITERATION * i + 1
        for k in range(CHUNKS_PER_ITERATION):
            step(first + k, (1 + k) % 2)
        return carry

    lax.fori_loop(0, n_loop, body, 0)
    for c in range(1 + n_loop * CHUNKS_PER_ITERATION, n_chunks - 1):
        step(c, c % 2)
    _interleave(score_pieces(n_chunks - 1, 1), tail_pieces(n_chunks - 2, 0))
    _interleave(tail_pieces(n_chunks - 1, 1))


def _chunk_rows(ci):
    start = ci * CHUNK
    if not isinstance(ci, int):
        start = pl.multiple_of(start, CHUNK)
    return pl.ds(start, CHUNK)


def _decay_pieces(log_decay, dmat_ref, d_ref, e_ref, elast_ref):
    c = CHUNK
    width = d_ref.shape[1]
    n_matmul_levels = dmat_ref.shape[0] // c - 1
    vpu_levels = LEVELS[:len(LEVELS) - n_matmul_levels]
    matmul_levels = LEVELS[len(LEVELS) - n_matmul_levels:]
    assert all(2 * s % SUBLANES == 0 for s in vpu_levels)
    d_ref[...] = _dot(dmat_ref[...], _split_bf16(log_decay * LOG2E))
    yield

    def put(blk, exponent):
        ex = jnp.exp2(exponent)
        e_ref[blk * c:(blk + 1) * c, :] = ex.astype(BF16)
        return ex

    elast_ref[...] = put(0, d_ref[0:c, :])[c - SUBLANES:c, :]
    yield
    b = d_ref[0:c, :]
    put(1, b[c - 1:c, :] - b)
    yield
    for s in vpu_levels:
        b = d_ref[0:c, :]
        ref = jnp.concatenate(
            [jnp.broadcast_to(b[g + s - 1:g + s, :], (2 * s, width)) for g in range(0, c, 2 * s)],
            axis=0)
        put(2 + LEVELS.index(s), -jnp.abs(b - ref))
        yield
    for n, s in enumerate(matmul_levels):
        put(2 + LEVELS.index(s), d_ref[(1 + n) * c:(2 + n) * c, :])
        yield


def _score_pieces(q_units, k_units, v_heads, decay_units, heads_per_unit, masks_ref, e_ref,
                  elast_ref, state_ref, o_ref):
    c = CHUNK
    n_heads = len(v_heads)
    lane = lax.broadcasted_iota(jnp.int32, (c, LANES), 1)
    head_w = LANES // heads_per_unit
    unit_lanes = [slice(u * LANES, (u + 1) * LANES) for u in range(len(q_units))]
    k16_units = [k.astype(BF16) for k in k_units]
    qf_heads, q16_heads, kown_heads = [], [], []
    for h in range(n_heads):
        u = h // heads_per_unit
        qf = q_units[u]
        k16 = k16_units[u]
        if heads_per_unit > 1:
            sub = h % heads_per_unit
            in_head = (lane >= sub * head_w) & (lane < (sub + 1) * head_w)
            qf = jnp.where(in_head, qf, 0.0)
            kown_heads.append(jnp.where(in_head, k16, jnp.zeros_like(k16)))
        else:
            kown_heads.append(k16)
        qf_heads.append(qf)
        q16_heads.append(qf.astype(BF16))
    v16_heads = [v.astype(BF16) for v in v_heads]
    yield

    odd_row = jnp.bitwise_and(lax.broadcasted_iota(jnp.int32, (c, 1), 0), 1) == 1
    k_prev_units = [pltpu.roll(k, 1, axis=0) for k in k_units]
    fine = []
    for h in range(n_heads):
        u = h // heads_per_unit
        s_diag = jnp.sum(qf_heads[h] * k_units[u], axis=-1, keepdims=True)
        s_pair = jnp.sum(qf_heads[h] * decay_units[u] * k_prev_units[u], axis=-1, keepdims=True)
        s_pair = jnp.where(odd_row, s_pair, 0.0)
        fine.append(s_diag * v_heads[h] + s_pair * pltpu.roll(v_heads[h], 1, axis=0))
    yield

    scores = [None] * n_heads
    for lvl in range(len(LEVELS)):
        for u, ul in enumerate(unit_lanes):
            e = e_ref[(2 + lvl) * c:(3 + lvl) * c, ul]
            unit_heads = range(u * heads_per_unit, (u + 1) * heads_per_unit)
            q_rows = jnp.concatenate([q16_heads[h] * e for h in unit_heads], axis=0)
            prod = _dot_nt(q_rows, k16_units[u] * e)
            for n, h in enumerate(unit_heads):
                term = masks_ref[lvl] * prod[n * c:(n + 1) * c, :]
                scores[h] = term if scores[h] is None else scores[h] + term
        yield

    states = [state_ref[h] for h in range(n_heads)]
    for h in range(n_heads):
        e_in = e_ref[0:c, unit_lanes[h // heads_per_unit]]
        o_ref[:, h * LANES:(h + 1) * LANES] = (
            _dot(scores[h].astype(BF16), v16_heads[h])
            + _dot(q16_heads[h] * e_in, states[h].astype(BF16)) + fine[h])
    yield
    for h in range(n_heads):
        lanes = unit_lanes[h // heads_per_unit]
        e_out = e_ref[c:2 * c, lanes]
        e_last = elast_ref[SUBLANES - 1:SUBLANES, lanes]
        e_col = jnp.transpose(jnp.broadcast_to(e_last, (LANES, LANES)))
        state_ref[h] = states[h] * e_col + _dot_tn(kown_heads[h] * e_out, v16_heads[h])
    yield


def _head_norm_gate(o, norm_w, gate):
    y = o * lax.rsqrt(jnp.mean(o * o, axis=-1, keepdims=True) + NORM_EPS) * norm_w
    return y * _silu(gate)


def _attention_scratch(ts, n_cols, decay_w, n_heads, dmat_rows):
    return [pltpu.VMEM((ts, n_cols), F32),
            pltpu.VMEM((dmat_rows, decay_w), F32),
            pltpu.VMEM((2, N_DECAY_BLOCKS * CHUNK, decay_w), BF16),
            pltpu.VMEM((2, SUBLANES, decay_w), F32),
            pltpu.VMEM((2, CHUNK, decay_w), F32),
            pltpu.VMEM((2, CHUNK, n_heads * LANES), F32),
            pltpu.VMEM((n_heads, LANES, LANES), F32)]


def _hgrn2_kernel(layer, z_ref, w_ref, lbl_ref, nw_ref, dmat_ref, masks_ref, y_ref,
                  proj_ref, d_ref, e_ref, elast_ref, dec_ref, o_ref, state_ref, key_ref):
    @pl.when(pl.program_id(1) == 0)
    def _():
        state_ref[...] = jnp.zeros_like(state_ref)

    proj_ref[...] = _dot_nt(z_ref[0], w_ref[0])

    logits = [lbl_ref[i:i + 1, :] for i in range(DEPTH)]
    mx = functools.reduce(jnp.maximum, logits)
    ex = [jnp.exp(t - mx) for t in logits]
    tot = functools.reduce(lambda a, b: a + b, ex)
    lb = functools.reduce(lambda a, b: a + b, ex[:layer + 1]) / tot - ex[0] / tot
    nw = nw_ref[...]
    w = BRANCH_W
    ts = proj_ref.shape[0]
    units = [slice(h * LANES, (h + 1) * LANES) for h in range(HG_HEADS)]

    def decay_pieces(ci, slot):
        rows = _chunk_rows(ci)
        zf = proj_ref[rows, w:2 * w]
        en = jnp.exp(-jnp.abs(zf))
        r = 1.0 / (1.0 + en)
        er = en * r
        pos = zf >= 0.0
        sig = jnp.where(pos, r, er)
        nsig = jnp.where(pos, er, r)
        key_ref[slot] = (1.0 - lb) * nsig
        forget = lb + (1.0 - lb) * sig
        dec_ref[slot] = forget
        log_f = jnp.log(forget)
        yield
        yield from _decay_pieces(log_f, dmat_ref, d_ref, e_ref.at[slot], elast_ref.at[slot])

    def score_pieces(ci, slot):
        rows = _chunk_rows(ci)
        q = proj_ref[rows, 0:w] * (HG_DK ** -0.5)
        key = key_ref[slot]
        forget = dec_ref[slot]
        vin = proj_ref[rows, 2 * w:3 * w]
        yield from _score_pieces([q[:, u] for u in units], [key[:, u] for u in units],
                                 [vin[:, u] for u in units], [forget[:, u] for u in units], 1,
                                 masks_ref, e_ref.at[slot], elast_ref.at[slot], state_ref,
                                 o_ref.at[slot])

    def tail_pieces(ci, slot):
        rows = _chunk_rows(ci)
        for h, u in enumerate(units):
            gate = proj_ref[rows, 3 * w + h * LANES:3 * w + (h + 1) * LANES]
            y_ref[0, rows, u] = _head_norm_gate(o_ref[slot, :, u], nw, gate).astype(BF16)
            yield

    _pipelined_chunks(ts // CHUNK, decay_pieces, score_pieces, tail_pieces)


def _hgrn2(z, w_all, lb_logits, norm_w, dmat, masks, layer, ts):
    bsz, seqlen, d = z.shape
    n_cols = _w_cols("hgrn2")
    return pl.pallas_call(
        functools.partial(_hgrn2_kernel, layer),
        grid=(bsz, seqlen // ts),
        in_specs=[_row_spec(ts, d), _w_spec("hgrn2", layer, d), _full_spec(lb_logits.shape),
                  _full_spec((1, LANES)), _full_spec(dmat.shape), _full_spec(masks.shape)],
        out_specs=_row_spec(ts, BRANCH_W),
        out_shape=jax.ShapeDtypeStruct((bsz, seqlen, BRANCH_W), BF16),
        scratch_shapes=_attention_scratch(ts, n_cols, BRANCH_W, HG_HEADS, dmat.shape[0])
                       + [pltpu.VMEM((2, CHUNK, BRANCH_W), F32)],
        compiler_params=_params(2),
        name="hgrn2_mixer",
    )(z, w_all, lb_logits, norm_w.reshape(1, LANES), dmat, masks)


def _gla_kernel(z_ref, w_ref, gkw_ref, gkb_ref, nw_ref, dmat_ref, masks_ref, y_ref,
                proj_ref, d_ref, e_ref, elast_ref, dec_ref, o_ref, state_ref):
    @pl.when(pl.program_id(1) == 0)
    def _():
        state_ref[...] = jnp.zeros_like(state_ref)

    proj_ref[...] = _dot_nt(z_ref[0], w_ref[0])
    nw = nw_ref[...]
    gkb = gkb_ref[...]
    qk_w = GLA_HEADS * GLA_DK
    w = BRANCH_W
    ts = proj_ref.shape[0]
    units = [slice(u * LANES, (u + 1) * LANES) for u in range(qk_w // LANES)]
    heads = [slice(h * LANES, (h + 1) * LANES) for h in range(GLA_HEADS)]
    gate_off = 2 * qk_w + w

    def decay_pieces(ci, slot):
        rows = _chunk_rows(ci)
        lowrank = proj_ref[rows, gate_off + w:gate_off + w + LANES]
        gk = _dot(lowrank.astype(BF16), gkw_ref[...]) + gkb
        log_alpha = (jnp.minimum(gk, 0.0) - jnp.log(1.0 + jnp.exp(-jnp.abs(gk)))) * (1.0 / GLA_TAU)
        dec_ref[slot] = jnp.exp(log_alpha)
        yield
        yield from _decay_pieces(log_alpha, dmat_ref, d_ref, e_ref.at[slot], elast_ref.at[slot])

    def score_pieces(ci, slot):
        rows = _chunk_rows(ci)
        q = proj_ref[rows, 0:qk_w] * (GLA_DK ** -0.5)
        k = proj_ref[rows, qk_w:2 * qk_w]
        v = proj_ref[rows, 2 * qk_w:2 * qk_w + w]
        alpha = dec_ref[slot]
        yield from _score_pieces([q[:, u] for u in units], [k[:, u] for u in units],
                                 [v[:, hs] for hs in heads], [alpha[:, u] for u in units],
                                 LANES // GLA_DK, masks_ref, e_ref.at[slot], elast_ref.at[slot],
                                 state_ref, o_ref.at[slot])

    def tail_pieces(ci, slot):
        rows = _chunk_rows(ci)
        for h, hs in enumerate(heads):
            gate = proj_ref[rows, gate_off + h * LANES:gate_off + (h + 1) * LANES]
            y_ref[0, rows, hs] = _head_norm_gate(o_ref[slot, :, hs], nw, gate).astype(BF16)
            yield

    _pipelined_chunks(ts // CHUNK, decay_pieces, score_pieces, tail_pieces)


def _gla(z, w_all, gk_w, gk_b, norm_w, dmat, masks, layer, ts):
    bsz, seqlen, d = z.shape
    n_cols = _w_cols("gla")
    qk_w = GLA_HEADS * GLA_DK
    return pl.pallas_call(
        _gla_kernel,
        grid=(bsz, seqlen // ts),
        in_specs=[_row_spec(ts, d), _w_spec("gla", layer, d), _full_spec((LANES, qk_w)),
                  _full_spec((1, qk_w)), _full_spec((1, LANES)), _full_spec(dmat.shape),
                  _full_spec(masks.shape)],
        out_specs=_row_spec(ts, BRANCH_W),
        out_shape=jax.ShapeDtypeStruct((bsz, seqlen, BRANCH_W), BF16),
        scratch_shapes=_attention_scratch(ts, n_cols, qk_w, GLA_HEADS, dmat.shape[0]),
        compiler_params=_params(2),
        name="gla_mixer",
    )(z, w_all, gk_w, gk_b.reshape(1, qk_w), norm_w.reshape(1, LANES), dmat, masks)


def _gelu(x):
    return 0.5 * x * (1.0 + lax.erf(x * (2.0 ** -0.5)))


def _sg_kernel(z_ref, w_ref, lnw_ref, lnb_ref, ws_ref, bias_ref, y_ref):
    w = BRANCH_W
    proj = _dot_nt(z_ref[0], w_ref[0])
    ts = proj.shape[0]
    u = _gelu(proj[:, 0:w])
    vf = _gelu(proj[:, w:2 * w])
    gate = proj[:, 2 * w:3 * w]
    mu = jnp.mean(vf, axis=-1, keepdims=True)
    dv = vf - mu
    var = jnp.mean(dv * dv, axis=-1, keepdims=True)
    vn = (dv * lax.rsqrt(var + NORM_EPS) * lnw_ref[...] + lnb_ref[...]).astype(BF16)
    ri = lax.broadcasted_iota(jnp.int32, (SG_LEN, SG_LEN), 0)
    cj = lax.broadcasted_iota(jnp.int32, (SG_LEN, SG_LEN), 1)
    allowed = (cj < CHUNK) | (ri >= CHUNK)
    wm = [jnp.where(allowed, ws_ref[g], 0.0).astype(BF16) for g in range(SG_GROUPS)]
    bias = bias_ref[...]
    ug = u * _silu(gate)
    for blk in range(ts // SG_LEN):
        rows = slice(blk * SG_LEN, (blk + 1) * SG_LEN)
        for g in range(SG_GROUPS):
            cols = slice(g * LANES, (g + 1) * LANES)
            mixed = _dot(wm[g], vn[rows, cols]) + bias[:, cols]
            y_ref[0, rows, cols] = (ug[rows, cols] * mixed).astype(BF16)


def _spatial_gating(z, w_all, ln_w, ln_b, w_s, bias, layer, ts):
    bsz, seqlen, d = z.shape
    assert _w_cols("sg") == 3 * BRANCH_W
    return pl.pallas_call(
        _sg_kernel,
        grid=(bsz, seqlen // ts),
        in_specs=[_row_spec(ts, d), _w_spec("sg", layer, d), _full_spec((1, BRANCH_W)),
                  _full_spec((1, BRANCH_W)), _full_spec(w_s.shape), _full_spec(bias.shape)],
        out_specs=_row_spec(ts, BRANCH_W),
        out_shape=jax.ShapeDtypeStruct((bsz, seqlen, BRANCH_W), BF16),
        compiler_params=_params(2),
        name="spatial_gating_mixer",
    )(z, w_all, ln_w.reshape(1, BRANCH_W), ln_b.reshape(1, BRANCH_W), w_s, bias)


def _lru_kernel(z_ref, w_ref, cw_ref, cb_ref, wax_ref, ba_ref, bx_ref, lam_ref, y_ref,
                xtb_ref, hs_ref, hprev_ref):
    bsz, tt, d = z_ref.shape
    w = BRANCH_W
    n_rows = bsz * tt
    n_slabs = w // LANES
    tail = (CONV_WIDTH - 1) * bsz

    @pl.when(pl.program_id(0) == 0)
    def _():
        xtb_ref[:, 0:tail, :] = jnp.zeros((n_slabs, tail, LANES), F32)
        hprev_ref[...] = jnp.zeros_like(hprev_ref)

    proj = _dot_nt(z_ref[...].reshape(n_rows, d), w_ref[0])
    gate = proj[:, w:2 * w]
    for b in range(bsz):
        rows = slice(b * tt, (b + 1) * tt)
        for j in range(n_slabs):
            xtb_ref[j, pl.ds(tail + b, tt, stride=bsz), :] = proj[rows, j * LANES:(j + 1) * LANES]

    def delayed(back):
        start = tail - back * bsz
        return jnp.concatenate(
            [xtb_ref[j, start:start + n_rows, :] for j in range(n_slabs)], axis=1)

    xc = cb_ref[...]
    for tap in range(CONV_WIDTH):
        xc = xc + delayed(CONV_WIDTH - 1 - tap) * cw_ref[tap:tap + 1, :]
    for j in range(n_slabs):
        xtb_ref[j, 0:tail, :] = xtb_ref[j, n_rows:n_rows + tail, :]

    xc16 = xc.astype(BF16)
    ra, ix = [], []
    for h in range(LRU_HEADS):
        cols = slice(h * LRU_HD, (h + 1) * LRU_HD)
        both = _dot(xc16[:, cols], wax_ref[h])
        ra.append(both[:, 0:LRU_HD])
        ix.append(both[:, LRU_HD:2 * LRU_HD])
    ig = _sigmoid(jnp.concatenate(ix, axis=1) + bx_ref[...])
    nlam = -lam_ref[...]
    softplus = jnp.maximum(nlam, 0.0) + jnp.log(1.0 + jnp.exp(-jnp.abs(nlam)))
    half_scale = (-0.5 * RG_C) * softplus
    log_a = half_scale + half_scale * jnp.tanh(0.5 * (jnp.concatenate(ra, axis=1) + ba_ref[...]))
    a = jnp.exp(log_a)
    var = jnp.maximum(-jnp.tanh(log_a) * (a * a + 1.0), SQRT_EPS)
    bt = (var * lax.rsqrt(var)) * (ig * xc)

    hcur = [hprev_ref[j] for j in range(n_slabs)]
    for t in range(tt):
        step = slice(t * bsz, (t + 1) * bsz)
        for j in range(n_slabs):
            cols = slice(j * LANES, (j + 1) * LANES)
            hcur[j] = a[step, cols] * hcur[j] + bt[step, cols]
            hs_ref[j, step, :] = hcur[j]
    for j in range(n_slabs):
        hprev_ref[j] = hcur[j]

    hseq = jnp.concatenate(
        [jnp.concatenate([hs_ref[j, pl.ds(b, tt, stride=bsz), :] for j in range(n_slabs)], axis=1)
         for b in range(bsz)], axis=0)
    y_ref[...] = (hseq * _silu(gate)).astype(BF16).reshape(bsz, tt, w)


def _rglru(z, w_all, conv_w, conv_b, wax, b_a, b_x, lam, layer, tt):
    bsz, seqlen, d = z.shape
    w = BRANCH_W
    assert _w_cols("lru") == 2 * w
    assert bsz == SUBLANES, "the RG-LRU scan puts the batch on the vreg sublanes"
    n_slabs = w // LANES
    return pl.pallas_call(
        _lru_kernel,
        grid=(seqlen // tt,),
        in_specs=[pl.BlockSpec((bsz, tt, d), lambda i: (0, i, 0)), _w_spec("lru", layer, d),
                  _full_spec((CONV_WIDTH, w)), _full_spec((1, w)), _full_spec(wax.shape),
                  _full_spec((1, w)), _full_spec((1, w)), _full_spec((1, w))],
        out_specs=pl.BlockSpec((bsz, tt, w), lambda i: (0, i, 0)),
        out_shape=jax.ShapeDtypeStruct((bsz, seqlen, w), BF16),
        scratch_shapes=[pltpu.VMEM((n_slabs, bsz * (tt + CONV_WIDTH - 1), LANES), F32),
                        pltpu.VMEM((n_slabs, bsz * tt, LANES), F32),
                        pltpu.VMEM((n_slabs, bsz, LANES), F32)],
        compiler_params=_params(1),
        name="rglru_mixer",
    )(z, w_all, conv_w, conv_b.reshape(1, w), wax, b_a.reshape(1, w), b_x.reshape(1, w),
      lam.reshape(1, w))


MERGE_COLS = 256


def _merge_kernel(final, z_ref, h_ref, ya_ref, yb_ref, yc_ref, yd_ref, wmg_ref, wbr_ref,
                  wout_ref, nw_ref, *rest):
    if final:
        out_ref, merged_ref = rest
    else:
        hn_ref, zn_ref, merged_ref = rest
    zt = z_ref[0]
    ys = [ya_ref[0], yb_ref[0], yc_ref[0], yd_ref[0]]
    for n in range(D_MODEL // MERGE_COLS):
        cols = slice(n * MERGE_COLS, (n + 1) * MERGE_COLS)
        merged = None
        for b in range(N_BRANCH):
            gcols = slice(b * D_MODEL + n * MERGE_COLS, b * D_MODEL + (n + 1) * MERGE_COLS)
            gate = _sigmoid(_dot_nt(zt, wmg_ref[0, gcols, :]))
            term = gate * _dot(ys[b], wbr_ref[b, :, cols])
            merged = term if merged is None else merged + term
        merged_ref[:, cols] = merged.astype(BF16)
    out = h_ref[0] + _dot(merged_ref[...], wout_ref[...])
    normed = _rms_norm_rows(out, nw_ref[...])
    if final:
        out_ref[0] = normed
    else:
        hn_ref[0] = out
        zn_ref[0] = normed.astype(BF16)


def _merge(z, h, ys, w_all, w_br, w_out, next_norm_w, layer, final, ts):
    bsz, seqlen, d = h.shape
    assert _w_cols("merge") == N_BRANCH * d
    f32_out = jax.ShapeDtypeStruct((bsz, seqlen, d), F32)
    if final:
        out_shape, out_specs = f32_out, _row_spec(ts, d)
    else:
        out_shape = (f32_out, jax.ShapeDtypeStruct((bsz, seqlen, d), BF16))
        out_specs = (_row_spec(ts, d), _row_spec(ts, d))
    return pl.pallas_call(
        functools.partial(_merge_kernel, final),
        grid=(bsz, seqlen // ts),
        in_specs=[_row_spec(ts, d), _row_spec(ts, d)] + [_row_spec(ts, BRANCH_W)] * N_BRANCH
                 + [_w_spec("merge", layer, d),
                    pl.BlockSpec((None,) + w_br.shape[1:], lambda *_: (layer, 0, 0, 0),
                                 pipeline_mode=pl.Buffered(1)),
                    pl.BlockSpec((None,) + w_out.shape[1:], lambda *_: (layer, 0, 0),
                                 pipeline_mode=pl.Buffered(1)),
                    _full_spec((1, d))],
        out_specs=out_specs,
        out_shape=out_shape,
        scratch_shapes=[pltpu.VMEM((ts, d), BF16)],
        compiler_params=_params(2),
        name="merge_final" if final else "merge",
    )(z, h, *ys, w_all, w_br, w_out, next_norm_w.reshape(1, d))


HGRN2_MATMUL_LEVELS = 4
GLA_MATMUL_LEVELS = 2
TS_NORM = 1024
TS_ATTN = 2048
TS_SG = 1024
TT_LRU = 128
TS_MERGE = 1024


def kernel(x, norm_w, w_in, hg_lb_logits, hg_norm_w, gla_gk_w, gla_gk_b, gla_norm_w, sg_ln_w, sg_ln_b, sg_w, sg_b, lru_conv_w, lru_conv_b, lru_w_a, lru_b_a, lru_w_x, lru_b_x, lru_lambda, w_branch, w_out, final_norm_w):
    dmat_hgrn2 = jnp.asarray(_decay_matrix(HGRN2_MATMUL_LEVELS), BF16)
    dmat_gla = jnp.asarray(_decay_matrix(GLA_MATMUL_LEVELS), BF16)
    masks = jnp.asarray(_level_masks(), F32)
    gkw_pad = jnp.zeros((LANES - GLA_RANK, GLA_HEADS * GLA_DK), BF16)
    w_all = _transposed_in_proj(w_in)
    w_br = w_branch.astype(BF16)
    w_o = w_out.astype(BF16)

    z = _first_norm(x, norm_w[0], TS_NORM)
    h = x
    for l in range(DEPTH):
        gkw = jnp.concatenate([gla_gk_w[l].astype(BF16), gkw_pad], axis=0)
        wax = jnp.concatenate([lru_w_a[l], lru_w_x[l]], axis=-1).astype(BF16)
        sg_bias = jnp.repeat(sg_b[l].T, LANES, axis=1)

        y_a = _hgrn2(z, w_all, hg_lb_logits, hg_norm_w[l], dmat_hgrn2, masks, l, TS_ATTN)
        y_b = _gla(z, w_all, gkw, gla_gk_b[l], gla_norm_w[l], dmat_gla, masks, l, TS_ATTN)
        y_c = _spatial_gating(z, w_all, sg_ln_w[l], sg_ln_b[l], sg_w[l], sg_bias, l, TS_SG)
        y_d = _rglru(z, w_all, lru_conv_w[l], lru_conv_b[l], wax, lru_b_a[l], lru_b_x[l],
                     lru_lambda[l], l, TT_LRU)
        ys = (y_a, y_b, y_c, y_d)
        if l + 1 < DEPTH:
            h, z = _merge(z, h, ys, w_all, w_br, w_o, norm_w[l + 1], l, False, TS_MERGE)
        else:
            return _merge(z, h, ys, w_all, w_br, w_o, final_norm_w, l, True, TS_MERGE)
```
